```python
import math
import jax, jax.numpy as jnp
from jax import lax
import numpy as np

D_MODEL = 4096
BATCH = 4
SEQ = 2048
DEPTH = 4
DEC_BATCH = 128
DEC_SEQ = 1
PAST_LEN = 8192
PAGE_SIZE = 128

N_AB = (DEPTH + 1) // 2
N_MLA = DEPTH // 2

GLA_HEADS = 4
GLA_HEAD_K = D_MODEL // 16
GLA_HEAD_V = D_MODEL // 8
GLA_K = GLA_HEADS * GLA_HEAD_K
GLA_V = GLA_HEADS * GLA_HEAD_V
GLA_GATE_RANK = 16
GLA_GATE_TAU = 16.0

DN_HEADS = 16
DN_HEAD_K = 128
DN_HEAD_V = 128
DN_K = DN_HEADS * DN_HEAD_K
DN_V = DN_HEADS * DN_HEAD_V
CONV_WIDTH = 4
DN_CONV_CH = 2 * DN_K + DN_V

AB_SPLITS = (GLA_K, GLA_K, GLA_V, GLA_GATE_RANK, GLA_V, DN_CONV_CH, DN_V, DN_HEADS, DN_HEADS)
AB_IN = sum(AB_SPLITS)
AB_MIX = GLA_V + DN_V
CHUNK = 64

MLA_HEADS = 32
Q_LORA = 1024
KV_LORA = 512
QK_NOPE = 128
QK_ROPE = 64
V_HEAD = 128
MLA_A = Q_LORA + KV_LORA + QK_ROPE
ROPE_THETA = 10000.0
Q_BLOCK = 128

D_FF = -(-(8 * D_MODEL) // (3 * 256)) * 256

RMS_EPS = 1e-6
L2_EPS = 1e-6

kernel_name = 'hybrid_gla_deltanet_mla_step'


def _rmsnorm(x, w):
    xf = x.astype(jnp.float32)
    y = xf * lax.rsqrt(jnp.mean(xf * xf, axis=-1, keepdims=True) + RMS_EPS)
    return (y * w.astype(jnp.float32)).astype(x.dtype)


def _l2norm(x):
    return x * lax.rsqrt(jnp.sum(x * x, axis=-1, keepdims=True) + L2_EPS)


def _rope(x, pos):
    half = QK_ROPE // 2
    inv = ROPE_THETA ** (-jnp.arange(half, dtype=jnp.float32) / half)
    ang = pos.astype(jnp.float32)[:, None] * inv[None, :]
    shape = (ang.shape[0],) + (1,) * (x.ndim - 3) + (half,)
    cos = jnp.cos(ang).reshape(shape)
    sin = jnp.sin(ang).reshape(shape)
    xf = x.astype(jnp.float32)
    x1, x2 = xf[..., :half], xf[..., half:]
    return jnp.concatenate([x1 * cos - x2 * sin, x2 * cos + x1 * sin], axis=-1).astype(x.dtype)


def _to_chunks(x, c):
    b, l = x.shape[:2]
    n = -(-l // c)
    x = jnp.pad(x, [(0, 0), (0, n * c - l)] + [(0, 0)] * (x.ndim - 2))
    return jnp.moveaxis(x.reshape((b, n, c) + x.shape[2:]), 1, 0)


def _from_chunks(y, l):
    n, b, c = y.shape[:3]
    return jnp.moveaxis(y, 0, 1).reshape((b, n * c) + y.shape[3:])[:, :l]


def _gla_chunked(q, k, v, g, s0):
    l = q.shape[1]
    c = min(CHUNK, l)
    causal = jnp.tril(jnp.ones((c, c), bool))
    def step(s, blk):
        qb, kb, vb, gb = blk
        cum = jnp.cumsum(gb, axis=1)
        diff = cum[:, :, None] - cum[:, None, :]
        dec = jnp.exp(jnp.where(causal[None, :, :, None, None], diff, -jnp.inf))
        att = jnp.einsum('bthd,bshd,btshd->bhts', qb, kb, dec)
        o = jnp.einsum('bhts,bshv->bthv', att, vb) + jnp.einsum('bthd,bhdv->bthv', qb * jnp.exp(cum), s)
        last = cum[:, -1]
        s_new = s * jnp.exp(last)[..., None] + jnp.einsum('bshd,bshv->bhdv', kb * jnp.exp(last[:, None] - cum), vb)
        return s_new, o
    s_fin, o = lax.scan(step, s0, (_to_chunks(q, c), _to_chunks(k, c), _to_chunks(v, c), _to_chunks(g, c)))
    return _from_chunks(o, l), s_fin


def _gated_delta_chunked(q, k, v, g, beta, s0):
    l = q.shape[1]
    c = min(CHUNK, l)
    dv = v.shape[-1]
    strict = jnp.tril(jnp.ones((c, c), bool), -1)
    causal = jnp.tril(jnp.ones((c, c), bool))
    eye = jnp.eye(c, dtype=jnp.float32)
    def step(s, blk):
        qb, kb, vb, gb, bb = blk
        cum = jnp.cumsum(gb, axis=1)
        diff = jnp.transpose(cum[:, :, None, :] - cum[:, None, :, :], (0, 3, 1, 2))
        dec_strict = jnp.exp(jnp.where(strict, diff, -jnp.inf))
        dec_causal = jnp.exp(jnp.where(causal, diff, -jnp.inf))
        kbeta = kb * bb[..., None]
        a = eye + jnp.einsum('bthd,bshd->bhts', kbeta, kb) * dec_strict
        rhs = jnp.concatenate([jnp.swapaxes(vb * bb[..., None], 1, 2),
                               jnp.swapaxes(kbeta * jnp.exp(cum)[..., None], 1, 2)], axis=-1)
        sol = lax.linalg.triangular_solve(a, rhs, left_side=True, lower=True)
        u, w = sol[..., :dv], sol[..., dv:]
        v_new = u - jnp.einsum('bhtd,bhdv->bhtv', w, s)
        att = jnp.einsum('bthd,bshd->bhts', qb, kb) * dec_causal
        o = jnp.einsum('bhts,bhsv->bthv', att, v_new) + jnp.einsum('bthd,bhdv->bthv', qb * jnp.exp(cum)[..., None], s)
        last = cum[:, -1]
        s_new = s * jnp.exp(last)[..., None, None] + jnp.einsum('bshd,bhsv->bhdv', kb * jnp.exp(last[:, None] - cum)[..., None], v_new)
        return s_new, o
    xs = (_to_chunks(q, c), _to_chunks(k, c), _to_chunks(v, c), _to_chunks(g, c), _to_chunks(beta, c))
    s_fin, o = lax.scan(step, s0, xs)
    return _from_chunks(o, l), s_fin


def _causal_conv(x, buf, w):
    l = x.shape[1]
    xp = jnp.concatenate([buf.astype(x.dtype), x], axis=1)
    y = xp[:, 0:l] * w[0]
    for i in range(1, CONV_WIDTH):
        y = y + xp[:, i:i + l] * w[i]
    return y, xp[:, xp.shape[1] - (CONV_WIDTH - 1):].astype(buf.dtype)


def _ab_mixer(h, s_gla, s_dn, conv_buf, w_in, w_gate_up, b_gate, gla_norm, conv_w, a_log, dt_bias, dn_norm, w_out):
    f32 = jnp.float32
    b, l, _ = h.shape
    idx = [int(i) for i in np.cumsum(AB_SPLITS)[:-1]]
    gq, gk, gv, g_lr, g_out, dn_qkv, dn_z, dn_a, dn_b = jnp.split(jnp.einsum('bld,de->ble', h, w_in), idx, axis=-1)
    q = gq.astype(f32).reshape(b, l, GLA_HEADS, GLA_HEAD_K) * GLA_HEAD_K ** -0.5
    k = gk.astype(f32).reshape(b, l, GLA_HEADS, GLA_HEAD_K)
    v = gv.astype(f32).reshape(b, l, GLA_HEADS, GLA_HEAD_V)
    g = jax.nn.log_sigmoid(jnp.einsum('blr,rk->blk', g_lr, w_gate_up).astype(f32) + b_gate.astype(f32)) / GLA_GATE_TAU
    o_gla, s_gla_new = _gla_chunked(q, k, v, g.reshape(b, l, GLA_HEADS, GLA_HEAD_K), s_gla.astype(f32))
    o_gla = _rmsnorm(o_gla, gla_norm) * jax.nn.silu(g_out.astype(f32).reshape(b, l, GLA_HEADS, GLA_HEAD_V))
    conv_out, conv_new = _causal_conv(dn_qkv, conv_buf, conv_w)
    conv_out = jax.nn.silu(conv_out.astype(f32))
    q2 = _l2norm(conv_out[..., :DN_K].reshape(b, l, DN_HEADS, DN_HEAD_K)) * DN_HEAD_K ** -0.5
    k2 = _l2norm(conv_out[..., DN_K:2 * DN_K].reshape(b, l, DN_HEADS, DN_HEAD_K))
    v2 = conv_out[..., 2 * DN_K:].reshape(b, l, DN_HEADS, DN_HEAD_V)
    decay = -jnp.exp(a_log.astype(f32)) * jax.nn.softplus(dn_a.astype(f32) + dt_bias.astype(f32))
    beta = jax.nn.sigmoid(dn_b.astype(f32))
    o_dn, s_dn_new = _gated_delta_chunked(q2, k2, v2, decay, beta, s_dn.astype(f32))
    o_dn = _rmsnorm(o_dn, dn_norm) * jax.nn.silu(dn_z.astype(f32).reshape(b, l, DN_HEADS, DN_HEAD_V))
    o = jnp.concatenate([o_gla.reshape(b, l, GLA_V), o_dn.reshape(b, l, DN_V)], axis=-1).astype(h.dtype)
    return jnp.einsum('blf,fd->bld', o, w_out), s_gla_new.astype(s_gla.dtype), s_dn_new.astype(s_dn.dtype), conv_new


def _mla_project(h, pos, w_a, q_norm, kv_norm, w_qb):
    b, l, _ = h.shape
    a = jnp.einsum('bld,de->ble', h, w_a)
    c_q, c_kv, k_pe = a[..., :Q_LORA], a[..., Q_LORA:Q_LORA + KV_LORA], a[..., Q_LORA + KV_LORA:]
    latent = _rmsnorm(c_kv, kv_norm)
    k_rope = _rope(k_pe, pos)
    q = jnp.einsum('blr,re->ble', _rmsnorm(c_q, q_norm), w_qb).reshape(b, l, MLA_HEADS, QK_NOPE + QK_ROPE)
    return q[..., :QK_NOPE], _rope(q[..., QK_NOPE:], pos), latent, k_rope


def _mla_prefill(q_nope, q_pe, latent, k_rope, w_kvb):
    b, l = latent.shape[:2]
    kv = jnp.einsum('blr,re->ble', latent, w_kvb).reshape(b, l, MLA_HEADS, QK_NOPE + V_HEAD)
    k_nope, v = kv[..., :QK_NOPE], kv[..., QK_NOPE:]
    qb = min(Q_BLOCK, l)
    nb = l // qb
    scale = (QK_NOPE + QK_ROPE) ** -0.5
    kpos = jnp.arange(l)
    def block(args):
        i, qn, qr = args
        s = (jnp.einsum('bqhd,bkhd->bhqk', qn, k_nope).astype(jnp.float32)
             + jnp.einsum('bqhp,bkp->bhqk', qr, k_rope).astype(jnp.float32)) * scale
        qpos = i * qb + jnp.arange(qb)
        s = jnp.where(kpos[None, :] <= qpos[:, None], s, -jnp.inf)
        p = jax.nn.softmax(s, axis=-1).astype(v.dtype)
        return jnp.einsum('bhqk,bkhv->bqhv', p, v)
    qn_b = jnp.moveaxis(q_nope.reshape(b, nb, qb, MLA_HEADS, QK_NOPE), 1, 0)
    qr_b = jnp.moveaxis(q_pe.reshape(b, nb, qb, MLA_HEADS, QK_ROPE), 1, 0)
    o = lax.map(block, (jnp.arange(nb), qn_b, qr_b))
    return jnp.moveaxis(o, 0, 1).reshape(b, l, MLA_HEADS * V_HEAD)


def _mla_decode(q_nope, q_pe, latent, k_rope, lat_pool, rope_pool, layer, page_table, w_kvb):
    f32 = jnp.float32
    b, l = latent.shape[:2]
    scale = (QK_NOPE + QK_ROPE) ** -0.5
    w = w_kvb.astype(f32).reshape(KV_LORA, MLA_HEADS, QK_NOPE + V_HEAD)
    w_uk, w_uv = w[..., :QK_NOPE], w[..., QK_NOPE:]
    q_lat = jnp.einsum('blhd,rhd->blhr', q_nope.astype(f32), w_uk)
    q_r = q_pe.astype(f32)
    def scores(c, r):
        return (jnp.einsum('blhr,bkr->blhk', q_lat, c) + jnp.einsum('blhp,bkp->blhk', q_r, r)) * scale
    c_new, r_new = latent.astype(f32), k_rope.astype(f32)
    s = scores(c_new, r_new)
    causal = jnp.tril(jnp.ones((l, l), bool))
    s = jnp.where(causal[None, :, None, :], s, -jnp.inf)
    m0 = jnp.max(s, axis=-1)
    e0 = jnp.exp(s - m0[..., None])
    carry0 = (m0, jnp.sum(e0, axis=-1), jnp.einsum('blhk,bkr->blhr', e0, c_new))
    def step(carry, pages):
        m, den, acc = carry
        c = lat_pool[layer, pages].astype(f32)
        r = rope_pool[layer, pages].astype(f32)
        sc = scores(c, r)
        m_new = jnp.maximum(m, jnp.max(sc, axis=-1))
        corr = jnp.exp(m - m_new)
        e = jnp.exp(sc - m_new[..., None])
        return (m_new, den * corr + jnp.sum(e, axis=-1), acc * corr[..., None] + jnp.einsum('blhk,bkr->blhr', e, c)), None
    (m, den, acc), _ = lax.scan(step, carry0, page_table.T)
    o = jnp.einsum('blhr,rhv->blhv', acc / den[..., None], w_uv)
    return o.reshape(b, l, MLA_HEADS * V_HEAD)


def _swiglu(h, w_gate, w_up, w_down):
    g = jnp.einsum('bld,df->blf', h, w_gate)
    u = jnp.einsum('bld,df->blf', h, w_up)
    return jnp.einsum('blf,fd->bld', jax.nn.silu(g) * u, w_down)


def setup_inputs(seed: int = 0) -> dict:
    key = jax.random.key(seed)
    ks = jax.random.split(key, 32)
    f32 = jnp.float32
    def nrm(i, shape, scale):
        return jax.random.normal(ks[i], shape, f32) * scale
    def gain(i, shape):
        return 1.0 + nrm(i, shape, 0.01)
    n_pages = PAST_LEN // PAGE_SIZE
    n_pool = -(-(DEC_BATCH * n_pages * 5) // 4)
    page_table = jax.random.permutation(ks[7], n_pool)[:DEC_BATCH * n_pages].reshape(DEC_BATCH, n_pages).astype(jnp.int32)
    dt = jnp.exp(jax.random.uniform(ks[16], (N_AB, DN_HEADS), f32, math.log(1e-3), math.log(1e-1)))
    dt_bias = dt + jnp.log(-jnp.expm1(-dt))
    a_log = jnp.log(jax.random.uniform(ks[15], (N_AB, DN_HEADS), f32, 1.0, 16.0))
    return {
        'x_prompt': nrm(0, (BATCH, SEQ, D_MODEL), 1.0),
        'x_sample': nrm(1, (DEC_BATCH, DEC_SEQ, D_MODEL), 1.0),
        'state_gla': nrm(2, (N_AB, DEC_BATCH, GLA_HEADS, GLA_HEAD_K, GLA_HEAD_V), 0.5),
        'state_delta': nrm(3, (N_AB, DEC_BATCH, DN_HEADS, DN_HEAD_K, DN_HEAD_V), 0.5),
        'state_conv': nrm(4, (N_AB, DEC_BATCH, CONV_WIDTH - 1, DN_CONV_CH), 1.0),
        'cache_latent': nrm(5, (N_MLA, n_pool, PAGE_SIZE, KV_LORA), 1.0),
        'cache_k_rope': nrm(6, (N_MLA, n_pool, PAGE_SIZE, QK_ROPE), 1.0),
        'page_table': page_table,
        'norm_mix': gain(8, (DEPTH, D_MODEL)),
        'norm_ffn': gain(9, (DEPTH, D_MODEL)),
        'norm_final': gain(10, (D_MODEL,)),
        'ab_w_in': nrm(11, (N_AB, D_MODEL, AB_IN), D_MODEL ** -0.5),
        'gla_w_gate': nrm(12, (N_AB, GLA_GATE_RANK, GLA_K), GLA_GATE_RANK ** -0.5),
        'gla_b_gate': nrm(13, (N_AB, GLA_K), 0.1),
        'gla_norm': gain(14, (N_AB, GLA_HEAD_V)),
        'dn_conv_w': nrm(17, (N_AB, CONV_WIDTH, DN_CONV_CH), CONV_WIDTH ** -0.5),
        'dn_a_log': a_log,
        'dn_dt_bias': dt_bias,
        'dn_norm': gain(18, (N_AB, DN_HEAD_V)),
        'ab_w_out': nrm(19, (N_AB, AB_MIX, D_MODEL), AB_MIX ** -0.5),
        'mla_w_a': nrm(20, (N_MLA, D_MODEL, MLA_A), D_MODEL ** -0.5),
        'mla_q_norm': gain(21, (N_MLA, Q_LORA)),
        'mla_kv_norm': gain(22, (N_MLA, KV_LORA)),
        'mla_w_qb': nrm(23, (N_MLA, Q_LORA, MLA_HEADS * (QK_NOPE + QK_ROPE)), Q_LORA ** -0.5),
        'mla_w_kvb': nrm(24, (N_MLA, KV_LORA, MLA_HEADS * (QK_NOPE + V_HEAD)), KV_LORA ** -0.5),
        'mla_w_o': nrm(25, (N_MLA, MLA_HEADS * V_HEAD, D_MODEL), (MLA_HEADS * V_HEAD) ** -0.5),
        'ffn_w_gate': nrm(26, (DEPTH, D_MODEL, D_FF), D_MODEL ** -0.5),
        'ffn_w_up': nrm(27, (DEPTH, D_MODEL, D_FF), D_MODEL ** -0.5),
        'ffn_w_down': nrm(28, (DEPTH, D_FF, D_MODEL), D_FF ** -0.5),
    }


def reference(x_prompt, x_sample, state_gla, state_delta, state_conv, cache_latent, cache_k_rope, page_table,
              norm_mix, norm_ffn, norm_final, ab_w_in, gla_w_gate, gla_b_gate, gla_norm, dn_conv_w,
              dn_a_log, dn_dt_bias, dn_norm, ab_w_out, mla_w_a, mla_q_norm, mla_kv_norm, mla_w_qb,
              mla_w_kvb, mla_w_o, ffn_w_gate, ffn_w_up, ffn_w_down):
    bp, lp = x_prompt.shape[:2]
    ls = x_sample.shape[1]
    past_len = page_table.shape[1] * cache_latent.shape[2]
    pos_p = jnp.arange(lp)
    pos_s = past_len + jnp.arange(ls)
    hp, hs = x_prompt, x_sample
    gla_p, gla_s, dn_p, dn_s, cv_p, cv_s = [], [], [], [], [], []
    lat_p, lat_s, kr_p, kr_s = [], [], [], []
    for layer in range(DEPTH):
        j = layer // 2
        if layer % 2 == 0:
            ab = (ab_w_in[j], gla_w_gate[j], gla_b_gate[j], gla_norm[j], dn_conv_w[j], dn_a_log[j],
                  dn_dt_bias[j], dn_norm[j], ab_w_out[j])
            z_gla = jnp.zeros((bp,) + state_gla.shape[2:], state_gla.dtype)
            z_dn = jnp.zeros((bp,) + state_delta.shape[2:], state_delta.dtype)
            z_cv = jnp.zeros((bp,) + state_conv.shape[2:], state_conv.dtype)
            op, g1, d1, c1 = _ab_mixer(_rmsnorm(hp, norm_mix[layer]), z_gla, z_dn, z_cv, *ab)
            osm, g2, d2, c2 = _ab_mixer(_rmsnorm(hs, norm_mix[layer]), state_gla[j], state_delta[j], state_conv[j], *ab)
            gla_p.append(g1); gla_s.append(g2)
            dn_p.append(d1); dn_s.append(d2)
            cv_p.append(c1); cv_s.append(c2)
        else:
            ml = (mla_w_a[j], mla_q_norm[j], mla_kv_norm[j], mla_w_qb[j])
            qn, qr, lat, kr = _mla_project(_rmsnorm(hp, norm_mix[layer]), pos_p, *ml)
            op = jnp.einsum('blf,fd->bld', _mla_prefill(qn, qr, lat, kr, mla_w_kvb[j]), mla_w_o[j])
            qn2, qr2, lat2, kr2 = _mla_project(_rmsnorm(hs, norm_mix[layer]), pos_s, *ml)
            o2 = _mla_decode(qn2, qr2, lat2, kr2, cache_latent, cache_k_rope, j, page_table, mla_w_kvb[j])
            osm = jnp.einsum('blf,fd->bld', o2.astype(hs.dtype), mla_w_o[j])
            lat_p.append(lat); lat_s.append(lat2)
            kr_p.append(kr); kr_s.append(kr2)
        hp = hp + op
        hs = hs + osm
        hp = hp + _swiglu(_rmsnorm(hp, norm_ffn[layer]), ffn_w_gate[layer], ffn_w_up[layer], ffn_w_down[layer])
        hs = hs + _swiglu(_rmsnorm(hs, norm_ffn[layer]), ffn_w_gate[layer], ffn_w_up[layer], ffn_w_down[layer])
    y_prompt = _rmsnorm(hp, norm_final)
    y_sample = _rmsnorm(hs, norm_final)
    return (y_prompt, y_sample, jnp.stack(gla_p), jnp.stack(gla_s), jnp.stack(dn_p), jnp.stack(dn_s),
            jnp.stack(cv_p), jnp.stack(cv_s), jnp.stack(lat_p), jnp.stack(lat_s), jnp.stack(kr_p), jnp.stack(kr_s))
```

```python
import functools
import math

import numpy as np
import jax
import jax.numpy as jnp
from jax import lax
from jax.experimental import pallas as pl
from jax.experimental.pallas import tpu as pltpu

F32 = jnp.float32
BF16 = jnp.bfloat16
HIGHEST = lax.Precision.HIGHEST

GLA_HEADS = 4
GLA_GATE_RANK = 16
GLA_GATE_TAU = 16.0
DN_HEADS = 16
DN_HEAD = 128
CONV_WIDTH = 4
CHUNK = 64
MLA_HEADS = 32
Q_LORA = 1024
KV_LORA = 512
QK_NOPE = 128
QK_ROPE = 64
V_HEAD = 128
ROPE_THETA = 10000.0
RMS_EPS = 1e-6
L2_EPS = 1e-6

LANE = 128
VMEM_LIMIT_BYTES = 60 * 1024 * 1024

ROW_TILE = 1040
NORM_ROWS = 16
GLA_SUB = 16
DN_GROUP = 4
ATT_TQ = 512
DEC_PAGES = 8
SAMPLE_ROWS = 8


def _cparams(sem):
    return pltpu.CompilerParams(dimension_semantics=sem, vmem_limit_bytes=VMEM_LIMIT_BYTES)


def _silu(x):
    return x * jax.nn.sigmoid(x)


def _log_sigmoid(x):
    return jnp.minimum(x, 0.0) - jnp.log(1.0 + jnp.exp(-jnp.abs(x)))


def _softplus(x):
    return jnp.maximum(x, 0.0) + jnp.log(1.0 + jnp.exp(-jnp.abs(x)))


def _nt(a, b):
    return lax.dot_general(a, b, (((1,), (1,)), ((), ())), preferred_element_type=F32)


def _tn(a, b):
    return lax.dot_general(a, b, (((0,), (0,)), ((), ())), preferred_element_type=F32)


def _dotx(a, b):
    return jnp.dot(a, b, precision=HIGHEST, preferred_element_type=F32)


def _mm_kernel(*refs, has_norm, has_resid, n_w, stage):
    it = iter(refs)
    x_ref = next(it)
    g_ref = next(it) if has_norm else None
    w_refs = [next(it) for _ in range(n_w)]
    r_ref = next(it) if has_resid else None
    o_ref = next(it)
    xs_ref = next(it) if stage else None

    if stage:
        @pl.when(pl.program_id(1) == 0)
        def _():
            def body(r, carry):
                sl = pl.ds(pl.multiple_of(r * NORM_ROWS, NORM_ROWS), NORM_ROWS)
                x = x_ref[sl, :].astype(F32)
                if has_norm:
                    x = x * lax.rsqrt(jnp.mean(x * x, axis=-1, keepdims=True) + RMS_EPS) * g_ref[...]
                xs_ref[sl, :] = x.astype(BF16)
                return carry
            lax.fori_loop(0, x_ref.shape[0] // NORM_ROWS, body, 0)
        xb = xs_ref[...]
    else:
        xb = x_ref[...]
    acc = [jnp.dot(xb, w[...].astype(BF16), preferred_element_type=F32) for w in w_refs]
    y = _silu(acc[0]) * acc[1] if n_w == 2 else acc[0]
    if has_resid:
        y = y + r_ref[...]
    o_ref[...] = y.astype(o_ref.dtype)


def _matmul(x, ws, *, tn, out_dtype, gain=None, resid=None, name):
    m = x.shape[0]
    k, n = ws[0].shape
    tm = ROW_TILE
    assert m % tm == 0 and n % tn == 0 and k % LANE == 0 and tm % NORM_ROWS == 0
    stage = gain is not None or x.dtype != BF16
    in_specs = [pl.BlockSpec((tm, k), lambda i, j: (i, 0), pipeline_mode=pl.Buffered(1))]
    args = [x]
    if gain is not None:
        in_specs.append(pl.BlockSpec((1, k), lambda i, j: (0, 0)))
        args.append(gain.reshape(1, k).astype(F32))
    for w in ws:
        in_specs.append(pl.BlockSpec((k, tn), lambda i, j: (0, j)))
        args.append(w)
    if resid is not None:
        in_specs.append(pl.BlockSpec((tm, tn), lambda i, j: (i, j)))
        args.append(resid)
    scratch = [pltpu.VMEM((tm, k), BF16)] if stage else []
    return pl.pallas_call(
        functools.partial(_mm_kernel, has_norm=gain is not None, has_resid=resid is not None,
                          n_w=len(ws), stage=stage),
        out_shape=jax.ShapeDtypeStruct((m, n), out_dtype),
        grid=(m // tm, n // tn),
        in_specs=in_specs,
        out_specs=pl.BlockSpec((tm, tn), lambda i, j: (i, j)),
        scratch_shapes=scratch,
        compiler_params=_cparams(("parallel", "arbitrary")),
        name=name,
    )(*args)


def _rmsnorm_kernel(x_ref, g_ref, o_ref):
    x = x_ref[...]
    o_ref[...] = x * lax.rsqrt(jnp.mean(x * x, axis=-1, keepdims=True) + RMS_EPS) * g_ref[...]


def _rmsnorm_rows(x, gain, *, rows=208):
    m, d = x.shape
    assert m % rows == 0
    return pl.pallas_call(
        _rmsnorm_kernel,
        out_shape=jax.ShapeDtypeStruct((m, d), F32),
        grid=(m // rows,),
        in_specs=[pl.BlockSpec((rows, d), lambda i: (i, 0)), pl.BlockSpec((1, d), lambda i: (0, 0))],
        out_specs=pl.BlockSpec((rows, d), lambda i: (i, 0)),
        compiler_params=_cparams(("parallel",)),
        name="final_rmsnorm",
    )(x, gain.reshape(1, d))


def _gla_gate(sm, wg_ref, bg_ref):
    gp = jnp.dot(sm.astype(BF16), wg_ref[...].astype(BF16), preferred_element_type=F32) + bg_ref[...]
    return _log_sigmoid(gp) / GLA_GATE_TAU


def _gated_head_norm(o, nw_ref, gate):
    on = o * lax.rsqrt(jnp.mean(o * o, axis=-1, keepdims=True) + RMS_EPS) * nw_ref[...]
    return on * _silu(gate)


def _gla_prompt_kernel(q_ref, k_ref, v_ref, go_ref, sm_ref, wg_ref, bg_ref, nw_ref,
                       o_ref, st_ref, sT_ref):
    c = pl.program_id(2)
    C, dk = q_ref.shape
    sub = GLA_SUB

    @pl.when(c == 0)
    def _():
        sT_ref[...] = jnp.zeros_like(sT_ref)

    q = q_ref[...] * (dk ** -0.5)
    k = k_ref[...]
    v = v_ref[...].astype(BF16)
    g = _gla_gate(sm_ref[...], wg_ref, bg_ref)
    r_i = lax.broadcasted_iota(jnp.int32, (C, C), 0)
    c_i = lax.broadcasted_iota(jnp.int32, (C, C), 1)
    cum = _dotx((r_i >= c_i).astype(F32), g)

    key_row = lax.broadcasted_iota(jnp.int32, (C, 1), 0)
    lane = lax.broadcasted_iota(jnp.int32, (sub, C), 1)
    rloc = lax.broadcasted_iota(jnp.int32, (sub, C), 0)
    blocks = []
    for i in range(C // sub):
        r0 = i * sub
        qi = q[r0:r0 + sub]
        cumi = cum[r0:r0 + sub]
        dg = jnp.zeros((sub, C), F32)
        for s in range(r0, r0 + sub):
            x = qi * k[s:s + 1] * jnp.exp(jnp.minimum(cumi - cum[s:s + 1], 0.0))
            dg = jnp.where(lane == s, jnp.sum(x, axis=1, keepdims=True), dg)
        att_i = jnp.where(rloc + r0 >= lane, dg, 0.0)
        if i > 0:
            ci = cum[r0 - 1:r0]
            qt = qi * jnp.exp(cumi - ci)
            kt = jnp.where(key_row < r0, k * jnp.exp(jnp.minimum(ci - cum, 0.0)), 0.0)
            att_i = att_i + _nt(qt.astype(BF16), kt.astype(BF16))
        blocks.append(att_i)
    att = jnp.concatenate(blocks, axis=0)

    sT = sT_ref[...]
    last = cum[C - 1:C]
    o = jnp.dot(att.astype(BF16), v, preferred_element_type=F32)
    o = o + _nt((q * jnp.exp(cum)).astype(BF16), sT.astype(BF16))
    kd = k * jnp.exp(last - cum)
    sT_new = sT * jnp.exp(last) + _tn(v, kd.astype(BF16))
    sT_ref[...] = sT_new
    o_ref[...] = _gated_head_norm(o, nw_ref, go_ref[...]).astype(o_ref.dtype)

    @pl.when(c == pl.num_programs(2) - 1)
    def _():
        st_ref[0, 0] = sT_new.T


def _gla_prompt(proj, cols, wg_pad, b_gate, gla_norm, *, batch, seq, dk, dv):
    nc = seq // CHUNK
    assert seq % CHUNK == 0
    h = GLA_HEADS
    qb, kb, vb, gb, sb = (cols["gq"] // dk, cols["gk"] // dk, cols["gv"] // dv, cols["g_out"] // dv,
                          cols["small"] // LANE)
    row = lambda b, hh, c: b * nc + c
    in_specs = [
        pl.BlockSpec((CHUNK, dk), lambda b, hh, c: (row(b, hh, c), qb + hh)),
        pl.BlockSpec((CHUNK, dk), lambda b, hh, c: (row(b, hh, c), kb + hh)),
        pl.BlockSpec((CHUNK, dv), lambda b, hh, c: (row(b, hh, c), vb + hh)),
        pl.BlockSpec((CHUNK, dv), lambda b, hh, c: (row(b, hh, c), gb + hh)),
        pl.BlockSpec((CHUNK, LANE), lambda b, hh, c: (row(b, hh, c), sb)),
        pl.BlockSpec((LANE, dk), lambda b, hh, c: (0, hh)),
        pl.BlockSpec((1, dk), lambda b, hh, c: (0, hh)),
        pl.BlockSpec((1, dv), lambda b, hh, c: (0, 0)),
    ]
    return pl.pallas_call(
        _gla_prompt_kernel,
        out_shape=(jax.ShapeDtypeStruct((batch * seq, h * dv), BF16),
                   jax.ShapeDtypeStruct((batch, h, dk, dv), F32)),
        grid=(batch, h, nc),
        in_specs=in_specs,
        out_specs=(pl.BlockSpec((CHUNK, dv), lambda b, hh, c: (row(b, hh, c), hh)),
                   pl.BlockSpec((1, 1, dk, dv), lambda b, hh, c: (b, hh, 0, 0))),
        scratch_shapes=[pltpu.VMEM((dv, dk), F32)],
        compiler_params=_cparams(("parallel", "parallel", "arbitrary")),
        name="gla_prompt",
    )(proj, proj, proj, proj, proj, wg_pad, b_gate.reshape(1, -1), gla_norm.reshape(1, -1))


def _gla_sample_kernel(q_ref, k_ref, v_ref, go_ref, sm_ref, wg_ref, bg_ref, nw_ref, s_ref,
                       o_ref, so_ref):
    nb, dk = q_ref.shape
    qT = (q_ref[...] * (dk ** -0.5)).T
    kT = k_ref[...].T
    egT = jnp.exp(_gla_gate(sm_ref[...], wg_ref, bg_ref)).T
    v = v_ref[...]
    rows = []
    for b in range(nb):
        s_new = s_ref[b, 0] * egT[:, b:b + 1] + kT[:, b:b + 1] * v[b:b + 1]
        so_ref[b, 0] = s_new
        rows.append(jnp.sum(qT[:, b:b + 1] * s_new, axis=0, keepdims=True))
    o = jnp.concatenate(rows, axis=0)
    o_ref[...] = _gated_head_norm(o, nw_ref, go_ref[...]).astype(o_ref.dtype)


def _gla_sample(proj, cols, wg_pad, b_gate, gla_norm, state, *, row0, dk, dv):
    nb = SAMPLE_ROWS
    bs, h = state.shape[:2]
    assert bs % nb == 0 and row0 % nb == 0
    r0 = row0 // nb
    qb, kb, vb, gb, sb = (cols["gq"] // dk, cols["gk"] // dk, cols["gv"] // dv, cols["g_out"] // dv,
                          cols["small"] // LANE)
    in_specs = [
        pl.BlockSpec((nb, dk), lambda i, hh: (r0 + i, qb + hh)),
        pl.BlockSpec((nb, dk), lambda i, hh: (r0 + i, kb + hh)),
        pl.BlockSpec((nb, dv), lambda i, hh: (r0 + i, vb + hh)),
        pl.BlockSpec((nb, dv), lambda i, hh: (r0 + i, gb + hh)),
        pl.BlockSpec((nb, LANE), lambda i, hh: (r0 + i, sb)),
        pl.BlockSpec((LANE, dk), lambda i, hh: (0, hh)),
        pl.BlockSpec((1, dk), lambda i, hh: (0, hh)),
        pl.BlockSpec((1, dv), lambda i, hh: (0, 0)),
        pl.BlockSpec((nb, 1, dk, dv), lambda i, hh: (i, hh, 0, 0)),
    ]
    return pl.pallas_call(
        _gla_sample_kernel,
        out_shape=(jax.ShapeDtypeStruct((bs, h * dv), BF16),
                   jax.ShapeDtypeStruct(state.shape, F32)),
        grid=(bs // nb, h),
        in_specs=in_specs,
        out_specs=(pl.BlockSpec((nb, dv), lambda i, hh: (i, hh)),
                   pl.BlockSpec((nb, 1, dk, dv), lambda i, hh: (i, hh, 0, 0))),
        compiler_params=_cparams(("parallel", "parallel")),
        name="gla_sample",
    )(proj, proj, proj, proj, proj, wg_pad, b_gate.reshape(1, -1), gla_norm.reshape(1, -1), state)


def _dn_gates(sm, alog_ref, dtb_ref):
    decay = -jnp.exp(alog_ref[...]) * _softplus(sm + dtb_ref[...])
    return decay, jax.nn.sigmoid(sm)


def _lane_col(x, lane_idx):
    lane = lax.broadcasted_iota(jnp.int32, x.shape, 1)
    return jnp.sum(jnp.where(lane == lane_idx, x, 0.0), axis=1, keepdims=True)


def _l2norm(x):
    return x * lax.rsqrt(jnp.sum(x * x, axis=-1, keepdims=True) + L2_EPS)


def _unit_lower_inverse(a, sub):
    C = a.shape[0]
    assert C // sub == 4
    r = lax.broadcasted_iota(jnp.int32, (C, C), 0)
    c = lax.broadcasted_iota(jnp.int32, (C, C), 1)
    eye = (r == c).astype(F32)
    same_blk = (r // sub) == (c // sub)
    x = jnp.where(same_blk, -a, 0.0)
    low = jnp.where(same_blk, 0.0, a)
    td = eye + x
    p = x
    steps = int(math.log2(sub)) - 1
    for _ in range(steps):
        p = _dotx(p, p)
        td = _dotx(td, eye + p)
    m = _dotx(td, low)
    tm = _dotx(eye - m, eye + _dotx(m, m))
    return _dotx(tm, td)


def _dn_prompt_kernel(q_ref, k_ref, v_ref, z_ref, sm_ref, wq_ref, wk_ref, wv_ref, alog_ref, dtb_ref,
                      nw_ref, o_ref, st_ref, s_ref, prev_ref, *, a_lane, b_lane):
    hg = pl.program_id(1)
    c = pl.program_id(2)
    C = q_ref.shape[0]
    G = s_ref.shape[0]
    d = DN_HEAD

    @pl.when(c == 0)
    def _():
        s_ref[...] = jnp.zeros_like(s_ref)
        prev_ref[...] = jnp.zeros_like(prev_ref)

    def conv(x_ref, w_ref, slot):
        x = x_ref[...]
        ext = jnp.concatenate([prev_ref[slot], x], axis=0)
        y = x * w_ref[CONV_WIDTH - 1:CONV_WIDTH]
        for i in range(1, CONV_WIDTH):
            y = y + pltpu.roll(ext, i, 0)[8:8 + C] * w_ref[CONV_WIDTH - 1 - i:CONV_WIDTH - i]
        prev_ref[slot] = x[C - 8:C]
        return _silu(y)

    yq = conv(q_ref, wq_ref, 0)
    yk = conv(k_ref, wk_ref, 1)
    yv = conv(v_ref, wv_ref, 2)
    sm = sm_ref[...]
    decay, beta_all = _dn_gates(sm, alog_ref, dtb_ref)
    r_i = lax.broadcasted_iota(jnp.int32, (C, C), 0)
    c_i = lax.broadcasted_iota(jnp.int32, (C, C), 1)
    cum_all = _dotx((r_i >= c_i).astype(F32), decay)
    eye = r_i == c_i
    z = z_ref[...]

    outs = []
    for hl in range(G):
        head = hg * G + hl
        sl = slice(hl * d, (hl + 1) * d)
        q2 = _l2norm(yq[:, sl]) * (d ** -0.5)
        k2 = _l2norm(yk[:, sl])
        v2 = yv[:, sl]
        cum = _lane_col(cum_all, a_lane + head)
        beta = _lane_col(beta_all, b_lane + head)
        cum_row = jnp.sum(jnp.where(eye, cum, 0.0), axis=0, keepdims=True)
        dec = jnp.exp(jnp.minimum(cum - cum_row, 0.0))
        kbeta = k2 * beta
        k2b = k2.astype(BF16)
        a = jnp.where(r_i > c_i, _nt(kbeta.astype(BF16), k2b) * dec, 0.0)
        t = _unit_lower_inverse(a, GLA_SUB)
        ecum = jnp.exp(cum)
        u = _dotx(t, v2 * beta)
        w = _dotx(t, kbeta * ecum)
        s = s_ref[hl]
        sb = s.astype(BF16)
        v_new = u - jnp.dot(w.astype(BF16), sb, preferred_element_type=F32)
        att = jnp.where(r_i >= c_i, _nt(q2.astype(BF16), k2b) * dec, 0.0)
        vnb = v_new.astype(BF16)
        o = jnp.dot(att.astype(BF16), vnb, preferred_element_type=F32)
        o = o + jnp.dot((q2 * ecum).astype(BF16), sb, preferred_element_type=F32)
        last = cum[C - 1:C]
        kd = k2 * jnp.exp(last - cum)
        s_new = s * jnp.exp(last) + _tn(kd.astype(BF16), vnb)
        s_ref[hl] = s_new
        outs.append(_gated_head_norm(o, nw_ref, z[:, sl]))
    o_ref[...] = jnp.concatenate(outs, axis=1).astype(o_ref.dtype)

    @pl.when(c == pl.num_programs(2) - 1)
    def _():
        st_ref[0] = s_ref[...]


def _dn_prompt(proj, cols, conv_w, alog_pad, dtb_pad, dn_norm, *, batch, seq):
    nc = seq // CHUNK
    assert seq % CHUNK == 0
    G, d = DN_GROUP, DN_HEAD
    gw = G * d
    ng = DN_HEADS // G
    dn_k = DN_HEADS * d
    qb, kb, vb, zb, sb = (cols["dn_q"] // gw, cols["dn_k"] // gw, cols["dn_v"] // gw, cols["dn_z"] // gw,
                          cols["small"] // LANE)
    row = lambda b, g, c: b * nc + c
    in_specs = [
        pl.BlockSpec((CHUNK, gw), lambda b, g, c: (row(b, g, c), qb + g)),
        pl.BlockSpec((CHUNK, gw), lambda b, g, c: (row(b, g, c), kb + g)),
        pl.BlockSpec((CHUNK, gw), lambda b, g, c: (row(b, g, c), vb + g)),
        pl.BlockSpec((CHUNK, gw), lambda b, g, c: (row(b, g, c), zb + g)),
        pl.BlockSpec((CHUNK, LANE), lambda b, g, c: (row(b, g, c), sb)),
        pl.BlockSpec((CONV_WIDTH, gw), lambda b, g, c: (0, g)),
        pl.BlockSpec((CONV_WIDTH, gw), lambda b, g, c: (0, dn_k // gw + g)),
        pl.BlockSpec((CONV_WIDTH, gw), lambda b, g, c: (0, 2 * dn_k // gw + g)),
        pl.BlockSpec((1, LANE), lambda b, g, c: (0, 0)),
        pl.BlockSpec((1, LANE), lambda b, g, c: (0, 0)),
        pl.BlockSpec((1, d), lambda b, g, c: (0, 0)),
    ]
    return pl.pallas_call(
        functools.partial(_dn_prompt_kernel, a_lane=cols["a_lane"], b_lane=cols["b_lane"]),
        out_shape=(jax.ShapeDtypeStruct((batch * seq, DN_HEADS * d), BF16),
                   jax.ShapeDtypeStruct((batch, DN_HEADS, d, d), F32)),
        grid=(batch, ng, nc),
        in_specs=in_specs,
        out_specs=(pl.BlockSpec((CHUNK, gw), lambda b, g, c: (row(b, g, c), g)),
                   pl.BlockSpec((1, G, d, d), lambda b, g, c: (b, g, 0, 0))),
        scratch_shapes=[pltpu.VMEM((G, d, d), F32), pltpu.VMEM((3, 8, gw), F32)],
        compiler_params=_cparams(("parallel", "parallel", "arbitrary")),
        name="deltanet_prompt",
    )(proj, proj, proj, proj, proj, conv_w, conv_w, conv_w, alog_pad, dtb_pad, dn_norm.reshape(1, -1))


def _dn_sample_kernel(q_ref, k_ref, v_ref, z_ref, sm_ref, bq_ref, bk_ref, bv_ref, wq_ref, wk_ref, wv_ref,
                      alog_ref, dtb_ref, nw_ref, s_ref, o_ref, so_ref, *, a_lane, b_lane):
    hg = pl.program_id(1)
    nb = q_ref.shape[0]
    G = s_ref.shape[1]
    d = DN_HEAD

    def conv(x_ref, buf_ref, w_ref):
        y = x_ref[...] * w_ref[CONV_WIDTH - 1:CONV_WIDTH]
        for i in range(CONV_WIDTH - 1):
            y = y + buf_ref[:, i, :] * w_ref[i:i + 1]
        return _silu(y)

    yq = conv(q_ref, bq_ref, wq_ref)
    yk = conv(k_ref, bk_ref, wk_ref)
    yv = conv(v_ref, bv_ref, wv_ref)
    decay, beta_all = _dn_gates(sm_ref[...], alog_ref, dtb_ref)
    z = z_ref[...]
    outs = []
    for hl in range(G):
        head = hg * G + hl
        sl = slice(hl * d, (hl + 1) * d)
        q2 = _l2norm(yq[:, sl]) * (d ** -0.5)
        k2 = _l2norm(yk[:, sl])
        v2 = yv[:, sl]
        eg = jnp.exp(_lane_col(decay, a_lane + head))
        beta = _lane_col(beta_all, b_lane + head)
        qT = q2.T
        kT = k2.T
        rows = []
        for b in range(nb):
            s = s_ref[b, hl]
            kc = kT[:, b:b + 1]
            ks = jnp.sum(kc * s, axis=0, keepdims=True)
            v_new = beta[b:b + 1] * (v2[b:b + 1] - eg[b:b + 1] * ks)
            s_new = s * eg[b:b + 1] + kc * v_new
            so_ref[b, hl] = s_new
            rows.append(jnp.sum(qT[:, b:b + 1] * s_new, axis=0, keepdims=True))
        outs.append(_gated_head_norm(jnp.concatenate(rows, axis=0), nw_ref, z[:, sl]))
    o_ref[...] = jnp.concatenate(outs, axis=1).astype(o_ref.dtype)


def _dn_sample(proj, cols, conv_w, alog_pad, dtb_pad, dn_norm, state, conv_buf, *, row0):
    nb = SAMPLE_ROWS
    bs = state.shape[0]
    G, d = DN_GROUP, DN_HEAD
    gw = G * d
    ng = DN_HEADS // G
    dn_k = DN_HEADS * d
    assert bs % nb == 0 and row0 % nb == 0
    r0 = row0 // nb
    qb, kb, vb, zb, sb = (cols["dn_q"] // gw, cols["dn_k"] // gw, cols["dn_v"] // gw, cols["dn_z"] // gw,
                          cols["small"] // LANE)
    nbuf = CONV_WIDTH - 1
    in_specs = [
        pl.BlockSpec((nb, gw), lambda i, g: (r0 + i, qb + g)),
        pl.BlockSpec((nb, gw), lambda i, g: (r0 + i, kb + g)),
        pl.BlockSpec((nb, gw), lambda i, g: (r0 + i, vb + g)),
        pl.BlockSpec((nb, gw), lambda i, g: (r0 + i, zb + g)),
        pl.BlockSpec((nb, LANE), lambda i, g: (r0 + i, sb)),
        pl.BlockSpec((nb, nbuf, gw), lambda i, g: (i, 0, g)),
        pl.BlockSpec((nb, nbuf, gw), lambda i, g: (i, 0, dn_k // gw + g)),
        pl.BlockSpec((nb, nbuf, gw), lambda i, g: (i, 0, 2 * dn_k // gw + g)),
        pl.BlockSpec((CONV_WIDTH, gw), lambda i, g: (0, g)),
        pl.BlockSpec((CONV_WIDTH, gw), lambda i, g: (0, dn_k // gw + g)),
        pl.BlockSpec((CONV_WIDTH, gw), lambda i, g: (0, 2 * dn_k // gw + g)),
        pl.BlockSpec((1, LANE), lambda i, g: (0, 0)),
        pl.BlockSpec((1, LANE), lambda i, g: (0, 0)),
        pl.BlockSpec((1, d), lambda i, g: (0, 0)),
        pl.BlockSpec((nb, G, d, d), lambda i, g: (i, g, 0, 0)),
    ]
    return pl.pallas_call(
        functools.partial(_dn_sample_kernel, a_lane=cols["a_lane"], b_lane=cols["b_lane"]),
        out_shape=(jax.ShapeDtypeStruct((bs, DN_HEADS * d), BF16),
                   jax.ShapeDtypeStruct(state.shape, F32)),
        grid=(bs // nb, ng),
        in_specs=in_specs,
        out_specs=(pl.BlockSpec((nb, gw), lambda i, g: (i, g)),
                   pl.BlockSpec((nb, G, d, d), lambda i, g: (i, g, 0, 0))),
        compiler_params=_cparams(("parallel", "parallel")),
        name="deltanet_sample",
    )(proj, proj, proj, proj, proj, conv_buf, conv_buf, conv_buf, conv_w, conv_w, conv_w,
      alog_pad, dtb_pad, dn_norm.reshape(1, -1), state)


def _mla_post_kernel(ckv_ref, kpe_ref, cs_ref, g_ref, lat_ref, kr_ref):
    x = ckv_ref[...]
    lat_ref[...] = x * lax.rsqrt(jnp.mean(x * x, axis=-1, keepdims=True) + RMS_EPS) * g_ref[...]
    zz = kpe_ref[...] * cs_ref[...]
    kr_ref[...] = zz + pltpu.roll(zz, QK_ROPE, 1)


def _mla_post(a, cs, kv_norm, *, rows=520):
    m = a.shape[0]
    assert m % rows == 0
    ckv_blk = Q_LORA // KV_LORA
    kpe_blk = (Q_LORA + KV_LORA) // LANE
    return pl.pallas_call(
        _mla_post_kernel,
        out_shape=(jax.ShapeDtypeStruct((m, KV_LORA), F32), jax.ShapeDtypeStruct((m, LANE), F32)),
        grid=(m // rows,),
        in_specs=[pl.BlockSpec((rows, KV_LORA), lambda i: (i, ckv_blk)),
                  pl.BlockSpec((rows, LANE), lambda i: (i, kpe_blk)),
                  pl.BlockSpec((rows, LANE), lambda i: (i, 0)),
                  pl.BlockSpec((1, KV_LORA), lambda i: (0, 0))],
        out_specs=(pl.BlockSpec((rows, KV_LORA), lambda i: (i, 0)),
                   pl.BlockSpec((rows, LANE), lambda i: (i, 0))),
        compiler_params=_cparams(("parallel",)),
        name="mla_latent_rope",
    )(a, a, cs, kv_norm.reshape(1, -1))


def _mla_prefill_kernel(qn_ref, qr_ref, cs_ref, kn_ref, v_ref, kr_ref, o_ref, *, scale):
    qi = pl.program_id(2)
    tq = qn_ref.shape[0]
    tk = tq
    q = jnp.concatenate([qn_ref[...].astype(BF16), (qr_ref[...] * cs_ref[...]).astype(BF16)], axis=1)
    qpos = qi * tq + lax.broadcasted_iota(jnp.int32, (tq, tk), 0)
    kloc = lax.broadcasted_iota(jnp.int32, (tq, tk), 1)

    def body(kj, carry):
        m, l, acc = carry
        ks = pl.ds(pl.multiple_of(kj * tk, tk), tk)
        kf = jnp.concatenate([kn_ref[ks, :], kr_ref[ks, :].astype(BF16)], axis=1)
        s = _nt(q, kf) * scale
        s = jnp.where(kj * tk + kloc <= qpos, s, -jnp.inf)
        m_new = jnp.maximum(m, jnp.max(s, axis=-1, keepdims=True))
        corr = jnp.exp(m - m_new)
        p = jnp.exp(s - m_new)
        l_new = l * corr + jnp.sum(p, axis=-1, keepdims=True)
        acc_new = acc * corr + jnp.dot(p.astype(BF16), v_ref[ks, :], preferred_element_type=F32)
        return m_new, l_new, acc_new

    init = (jnp.full((tq, 1), -jnp.inf, F32), jnp.zeros((tq, 1), F32), jnp.zeros((tq, V_HEAD), F32))
    m, l, acc = lax.fori_loop(0, qi + 1, body, init)
    o_ref[...] = (acc / l).astype(o_ref.dtype)


def _mla_prefill(q, cs, kv, kr2, *, batch, seq):
    tq = ATT_TQ
    assert seq % tq == 0
    nq = seq // tq
    h = MLA_HEADS
    scale = (QK_NOPE + QK_ROPE) ** -0.5
    return pl.pallas_call(
        functools.partial(_mla_prefill_kernel, scale=scale),
        out_shape=jax.ShapeDtypeStruct((batch * seq, h * V_HEAD), BF16),
        grid=(batch, h, nq),
        in_specs=[
            pl.BlockSpec((tq, QK_NOPE), lambda b, hh, i: (b * nq + i, hh)),
            pl.BlockSpec((tq, LANE), lambda b, hh, i: (b * nq + i, h + hh)),
            pl.BlockSpec((tq, LANE), lambda b, hh, i: (b * nq + i, 0)),
            pl.BlockSpec((seq, QK_NOPE), lambda b, hh, i: (b, 2 * hh)),
            pl.BlockSpec((seq, V_HEAD), lambda b, hh, i: (b, 2 * hh + 1)),
            pl.BlockSpec((seq, LANE), lambda b, hh, i: (b, 0)),
        ],
        out_specs=pl.BlockSpec((tq, V_HEAD), lambda b, hh, i: (b * nq + i, hh)),
        compiler_params=_cparams(("parallel", "parallel", "arbitrary")),
        name="mla_prefill",
    )(q, q, cs, kv, kv, kr2)


def _absorb_q_kernel(x_ref, w_ref, o_ref):
    o_ref[...] = _nt(x_ref[...].astype(BF16), w_ref[...].astype(BF16)).astype(o_ref.dtype)


def _absorb_q(q, w_kvb, *, row0, rows):
    h = MLA_HEADS
    assert row0 % rows == 0
    return pl.pallas_call(
        _absorb_q_kernel,
        out_shape=jax.ShapeDtypeStruct((rows, h * KV_LORA), BF16),
        grid=(h,),
        in_specs=[pl.BlockSpec((rows, QK_NOPE), lambda hh: (row0 // rows, hh)),
                  pl.BlockSpec((KV_LORA, QK_NOPE), lambda hh: (0, 2 * hh))],
        out_specs=pl.BlockSpec((rows, KV_LORA), lambda hh: (0, hh)),
        compiler_params=_cparams(("parallel",)),
        name="mla_absorb_q",
    )(q, w_kvb)


def _absorb_v_kernel(x_ref, w_ref, o_ref):
    o_ref[...] = jnp.dot(x_ref[...].astype(BF16), w_ref[...].astype(BF16),
                         preferred_element_type=F32).astype(o_ref.dtype)


def _absorb_v(o_lat, w_kvb):
    rows = o_lat.shape[0]
    h = MLA_HEADS
    return pl.pallas_call(
        _absorb_v_kernel,
        out_shape=jax.ShapeDtypeStruct((rows, h * V_HEAD), BF16),
        grid=(h,),
        in_specs=[pl.BlockSpec((rows, KV_LORA), lambda hh: (0, hh)),
                  pl.BlockSpec((KV_LORA, V_HEAD), lambda hh: (0, 2 * hh + 1))],
        out_specs=pl.BlockSpec((rows, V_HEAD), lambda hh: (0, hh)),
        compiler_params=_cparams(("parallel",)),
        name="mla_absorb_v",
    )(o_lat, w_kvb)


def _mla_decode_kernel(pt_ref, ql_ref, qr_ref, cs_ref, cn_ref, rn_ref, *rest, scale, n_pages):
    lat_refs = rest[:n_pages]
    rope_refs = rest[n_pages:2 * n_pages]
    o_ref, m_ref, l_ref, acc_ref = rest[2 * n_pages:]
    g = pl.program_id(1)
    ql = ql_ref[0]
    zz = qr_ref[0] * cs_ref[...]

    @pl.when(g == 0)
    def _():
        s0 = (jnp.sum(ql.astype(F32) * cn_ref[0], axis=-1, keepdims=True)
              + jnp.sum(zz * rn_ref[0], axis=-1, keepdims=True)) * scale
        m_ref[...] = s0
        l_ref[...] = jnp.ones_like(l_ref)
        acc_ref[...] = jnp.broadcast_to(cn_ref[0], acc_ref.shape)

    qrot = (zz + pltpu.roll(zz, QK_ROPE, 1))[:, :QK_ROPE].astype(BF16)
    m, l, acc = m_ref[...], l_ref[...], acc_ref[...]
    for i in range(n_pages):
        cb = lat_refs[i][0, 0].astype(BF16)
        rb = rope_refs[i][0, 0].astype(BF16)
        s = (_nt(ql, cb) + _nt(qrot, rb)) * scale
        m_new = jnp.maximum(m, jnp.max(s, axis=-1, keepdims=True))
        corr = jnp.exp(m - m_new)
        e = jnp.exp(s - m_new)
        l = l * corr + jnp.sum(e, axis=-1, keepdims=True)
        acc = acc * corr + jnp.dot(e.astype(BF16), cb, preferred_element_type=F32)
        m = m_new
    m_ref[...], l_ref[...], acc_ref[...] = m, l, acc

    @pl.when(g == pl.num_programs(1) - 1)
    def _():
        o_ref[0] = acc / l


def _mla_decode(q_lat, q_rope, cs_row, lat_new, kr_new, cache_latent, cache_k_rope, page_table, layer):
    bs = q_lat.shape[0]
    h = MLA_HEADS
    n_log = page_table.shape[1]
    page = cache_latent.shape[2]
    npg = DEC_PAGES
    assert n_log % npg == 0
    scale = (QK_NOPE + QK_ROPE) ** -0.5

    def page_spec(width, i):
        return pl.BlockSpec((1, 1, page, width),
                            lambda b, g, pt: (layer, pt[b * n_log + g * npg + i], 0, 0))

    in_specs = [
        pl.BlockSpec((1, h, KV_LORA), lambda b, g, pt: (b, 0, 0)),
        pl.BlockSpec((1, h, LANE), lambda b, g, pt: (b, 0, 0)),
        pl.BlockSpec((1, LANE), lambda b, g, pt: (0, 0)),
        pl.BlockSpec((1, 1, KV_LORA), lambda b, g, pt: (b, 0, 0)),
        pl.BlockSpec((1, 1, LANE), lambda b, g, pt: (b, 0, 0)),
    ]
    in_specs += [page_spec(KV_LORA, i) for i in range(npg)]
    in_specs += [page_spec(QK_ROPE, i) for i in range(npg)]
    grid_spec = pltpu.PrefetchScalarGridSpec(
        num_scalar_prefetch=1,
        grid=(bs, n_log // npg),
        in_specs=in_specs,
        out_specs=pl.BlockSpec((1, h, KV_LORA), lambda b, g, pt: (b, 0, 0)),
        scratch_shapes=[pltpu.VMEM((h, 1), F32), pltpu.VMEM((h, 1), F32), pltpu.VMEM((h, KV_LORA), F32)],
    )
    return pl.pallas_call(
        functools.partial(_mla_decode_kernel, scale=scale, n_pages=npg),
        out_shape=jax.ShapeDtypeStruct((bs, h, KV_LORA), F32),
        grid_spec=grid_spec,
        compiler_params=_cparams(("parallel", "arbitrary")),
        name="mla_decode",
    )(page_table.reshape(-1), q_lat.reshape(bs, h, KV_LORA), q_rope.reshape(bs, h, LANE), cs_row,
      lat_new.reshape(bs, 1, KV_LORA), kr_new.reshape(bs, 1, LANE),
      *([cache_latent] * npg), *([cache_k_rope] * npg))


def _rot_cols(w):
    half = QK_ROPE // 2
    return jnp.concatenate([-w[..., half:], w[..., :half]], axis=-1)


def _ab_weight_layout(w_in, d_model):
    gla_k = GLA_HEADS * (d_model // 16)
    gla_v = GLA_HEADS * (d_model // 8)
    dn = DN_HEADS * DN_HEAD
    splits = (gla_k, gla_k, gla_v, GLA_GATE_RANK, gla_v, 3 * dn, dn, DN_HEADS, DN_HEADS)
    idx = [0] + [int(i) for i in np.cumsum(splits)]
    gq, gk, gv, g_lr, g_out, dn_qkv, dn_z, dn_a, dn_b = [w_in[:, idx[i]:idx[i + 1]] for i in range(9)]
    n_small = GLA_GATE_RANK + 2 * DN_HEADS
    main = gla_k * 2 + gla_v * 2 + 4 * dn
    total = -(-(main + LANE) // 512) * 512
    pad = jnp.zeros((w_in.shape[0], total - main - n_small), w_in.dtype)
    w = jnp.concatenate([gq, gk, gv, g_out, dn_qkv, dn_z, g_lr, dn_a, dn_b, pad], axis=1)
    cols = dict(gq=0, gk=gla_k, gv=2 * gla_k, g_out=2 * gla_k + gla_v,
                dn_q=2 * gla_k + 2 * gla_v, dn_k=2 * gla_k + 2 * gla_v + dn,
                dn_v=2 * gla_k + 2 * gla_v + 2 * dn, dn_z=2 * gla_k + 2 * gla_v + 3 * dn,
                small=main, a_lane=GLA_GATE_RANK, b_lane=GLA_GATE_RANK + DN_HEADS)
    return w, cols


def _pad_lanes(v, lane0):
    return jnp.zeros((1, LANE), F32).at[0, lane0:lane0 + v.shape[0]].set(v.astype(F32))


def _rope_table(pos):
    half = QK_ROPE // 2
    inv = ROPE_THETA ** (-jnp.arange(half, dtype=F32) / half)
    ang = pos.astype(F32)[:, None] * inv[None, :]
    cos, sin = jnp.cos(ang), jnp.sin(ang)
    return jnp.concatenate([cos, cos, sin, sin], axis=1)


def _mla_mixer(x, cs, norm_w, w_a, q_norm, kv_norm, w_qb, w_kvb, w_o, cache_latent, cache_k_rope,
               page_table, j, *, bp, lp, bs):
    tp = bp * lp
    d_model = x.shape[1]
    kpe_w = w_a[:, Q_LORA + KV_LORA:]
    a_pad = jnp.zeros((d_model, -(Q_LORA + KV_LORA + LANE) % 512), F32)
    w_a2 = jnp.concatenate([w_a[:, :Q_LORA + KV_LORA], kpe_w, _rot_cols(kpe_w), a_pad], axis=1)
    a = _matmul(x, [w_a2], tn=512, out_dtype=F32, gain=norm_w, name="mla_a_proj")
    latent, kr2 = _mla_post(a, cs, kv_norm)
    w_qb = w_qb.reshape(Q_LORA, MLA_HEADS, QK_NOPE + QK_ROPE)
    w_rope = w_qb[..., QK_NOPE:]
    w_q2 = jnp.concatenate(
        [w_qb[..., :QK_NOPE].reshape(Q_LORA, -1),
         jnp.concatenate([w_rope, _rot_cols(w_rope)], axis=-1).reshape(Q_LORA, -1)], axis=1)
    q = _matmul(a, [w_q2], tn=512, out_dtype=F32, gain=q_norm, name="mla_q_proj")
    kv = _matmul(latent, [w_kvb], tn=512, out_dtype=BF16, name="mla_kv_proj")
    o_p = _mla_prefill(q, cs, kv, kr2, batch=bp, seq=lp)
    q_lat = _absorb_q(q, w_kvb, row0=tp, rows=bs)
    q_rope_s = q[tp:, MLA_HEADS * QK_NOPE:]
    o_lat = _mla_decode(q_lat, q_rope_s, cs[tp:tp + 1], latent[tp:], kr2[tp:],
                        cache_latent, cache_k_rope, page_table, j)
    o_s = _absorb_v(o_lat.reshape(bs, MLA_HEADS * KV_LORA), w_kvb)
    att = jnp.concatenate([o_p, o_s], axis=0)
    x = _matmul(att, [w_o], tn=512, out_dtype=F32, resid=x, name="mla_out_proj")
    return x, latent, kr2


def kernel(x_prompt, x_sample, state_gla, state_delta, state_conv, cache_latent, cache_k_rope, page_table,
           norm_mix, norm_ffn, norm_final, ab_w_in, gla_w_gate, gla_b_gate, gla_norm, dn_conv_w,
           dn_a_log, dn_dt_bias, dn_norm, ab_w_out, mla_w_a, mla_q_norm, mla_kv_norm, mla_w_qb,
           mla_w_kvb, mla_w_o, ffn_w_gate, ffn_w_up, ffn_w_down):
    bp, lp, d_model = x_prompt.shape
    bs, ls, _ = x_sample.shape
    assert ls == 1
    tp = bp * lp
    depth = norm_mix.shape[0]
    past_len = page_table.shape[1] * cache_latent.shape[2]
    gla_dk, gla_dv = state_gla.shape[3:]
    dn_ch = state_conv.shape[-1]

    x = jnp.concatenate([x_prompt.reshape(tp, d_model), x_sample.reshape(bs, d_model)], axis=0)
    pos = jnp.concatenate([jnp.tile(jnp.arange(lp), bp), jnp.full((bs,), past_len)])
    cs = _rope_table(pos)

    gla_p, gla_s, dn_p, dn_s, cv_p, cv_s = [], [], [], [], [], []
    lat_p, lat_s, kr_p, kr_s = [], [], [], []
    for layer in range(depth):
        j = layer // 2
        if layer % 2 == 0:
            w_in, cols = _ab_weight_layout(ab_w_in[j], d_model)
            proj = _matmul(x, [w_in], tn=512, out_dtype=F32, gain=norm_mix[layer], name="ab_in_proj")
            wg_pad = jnp.zeros((LANE, gla_w_gate.shape[2]), F32).at[:GLA_GATE_RANK].set(gla_w_gate[j])
            alog_pad = _pad_lanes(dn_a_log[j], cols["a_lane"])
            dtb_pad = _pad_lanes(dn_dt_bias[j], cols["a_lane"])
            og_p, g1 = _gla_prompt(proj, cols, wg_pad, gla_b_gate[j], gla_norm[j],
                                   batch=bp, seq=lp, dk=gla_dk, dv=gla_dv)
            og_s, g2 = _gla_sample(proj, cols, wg_pad, gla_b_gate[j], gla_norm[j], state_gla[j],
                                   row0=tp, dk=gla_dk, dv=gla_dv)
            od_p, d1 = _dn_prompt(proj, cols, dn_conv_w[j], alog_pad, dtb_pad, dn_norm[j], batch=bp, seq=lp)
            od_s, d2 = _dn_sample(proj, cols, dn_conv_w[j], alog_pad, dtb_pad, dn_norm[j],
                                  state_delta[j], state_conv[j], row0=tp)
            raw = proj[:, cols["dn_q"]:cols["dn_q"] + dn_ch]
            c1 = raw[:tp].reshape(bp, lp, dn_ch)[:, lp - (CONV_WIDTH - 1):]
            c2 = jnp.concatenate([state_conv[j][:, 1:], raw[tp:, None, :]], axis=1)
            mix = jnp.concatenate([jnp.concatenate([og_p, od_p], axis=1),
                                   jnp.concatenate([og_s, od_s], axis=1)], axis=0)
            x = _matmul(mix, [ab_w_out[j]], tn=512, out_dtype=F32, resid=x, name="ab_out_proj")
            gla_p.append(g1); gla_s.append(g2)
            dn_p.append(d1); dn_s.append(d2)
            cv_p.append(c1); cv_s.append(c2)
        else:
            x, latent, kr2 = _mla_mixer(x, cs, norm_mix[layer], mla_w_a[j], mla_q_norm[j], mla_kv_norm[j],
                                        mla_w_qb[j], mla_w_kvb[j], mla_w_o[j], cache_latent, cache_k_rope,
                                        page_table, j, bp=bp, lp=lp, bs=bs)
            lat_p.append(latent[:tp].reshape(bp, lp, KV_LORA)); lat_s.append(latent[tp:].reshape(bs, 1, KV_LORA))
            kr_p.append(kr2[:tp, :QK_ROPE].reshape(bp, lp, QK_ROPE)); kr_s.append(kr2[tp:, :QK_ROPE].reshape(bs, 1, QK_ROPE))
        hid = _matmul(x, [ffn_w_gate[layer], ffn_w_up[layer]], tn=256, out_dtype=BF16,
                      gain=norm_ffn[layer], name="ffn_gate_up")
        x = _matmul(hid, [ffn_w_down[layer]], tn=256, out_dtype=F32, resid=x, name="ffn_down")
    y = _rmsnorm_rows(x, norm_final)
    y_prompt = y[:tp].reshape(bp, lp, d_model)
    y_sample = y[tp:].reshape(bs, 1, d_model)
    return (y_prompt, y_sample, jnp.stack(gla_p), jnp.stack(gla_s), jnp.stack(dn_p), jnp.stack(dn_s),
            jnp.stack(cv_p), jnp.stack(cv_s), jnp.stack(lat_p), jnp.stack(lat_s), jnp.stack(kr_p), jnp.stack(kr_s))
```

```python
import functools
import math

import numpy as np
import jax
import jax.numpy as jnp
from jax import lax
from jax.experimental import pallas as pl
from jax.experimental.pallas import tpu as pltpu

F32 = jnp.float32
BF16 = jnp.bfloat16
HIGHEST = lax.Precision.HIGHEST

GLA_HEADS = 4
GLA_GATE_RANK = 16
GLA_GATE_TAU = 16.0
DN_HEADS = 16
DN_HEAD = 128
CONV_WIDTH = 4
CHUNK = 64
MLA_HEADS = 32
Q_LORA = 1024
KV_LORA = 512
QK_NOPE = 128
QK_ROPE = 64
V_HEAD = 128
ROPE_THETA = 10000.0
RMS_EPS = 1e-6
L2_EPS = 1e-6

LANE = 128
VMEM_LIMIT_BYTES = 60 * 1024 * 1024

ROW_TILE = 1040
NORM_ROWS = 16
GLA_SUB = 16
DN_GROUP = 8
ATT_TQ = 512
DEC_PAGES = 16
SAMPLE_ROWS = 8


def _cparams(sem):
    return pltpu.CompilerParams(dimension_semantics=sem, vmem_limit_bytes=VMEM_LIMIT_BYTES)


def _silu(x):
    return x * jax.nn.sigmoid(x)


def _log_sigmoid(x):
    return jnp.minimum(x, 0.0) - jnp.log(1.0 + jnp.exp(-jnp.abs(x)))


def _softplus(x):
    return jnp.maximum(x, 0.0) + jnp.log(1.0 + jnp.exp(-jnp.abs(x)))


def _nt(a, b):
    return lax.dot_general(a, b, (((1,), (1,)), ((), ())), preferred_element_type=F32)


def _tn(a, b):
    return lax.dot_general(a, b, (((0,), (0,)), ((), ())), preferred_element_type=F32)


def _dotx(a, b):
    return jnp.dot(a, b, precision=HIGHEST, preferred_element_type=F32)


def _mm_kernel(*refs, has_norm, has_resid, n_w, stage):
    it = iter(refs)
    x_ref = next(it)
    g_ref = next(it) if has_norm else None
    w_refs = [next(it) for _ in range(n_w)]
    r_ref = next(it) if has_resid else None
    o_ref = next(it)
    xs_ref = next(it) if stage else None

    if stage:
        @pl.when(pl.program_id(1) == 0)
        def _():
            def body(r, carry):
                sl = pl.ds(pl.multiple_of(r * NORM_ROWS, NORM_ROWS), NORM_ROWS)
                x = x_ref[sl, :].astype(F32)
                if has_norm:
                    x = x * lax.rsqrt(jnp.mean(x * x, axis=-1, keepdims=True) + RMS_EPS) * g_ref[...]
                xs_ref[sl, :] = x.astype(BF16)
                return carry
            lax.fori_loop(0, x_ref.shape[0] // NORM_ROWS, body, 0)
        xb = xs_ref[...]
    else:
        xb = x_ref[...]
    acc = [jnp.dot(xb, w[...].astype(BF16), preferred_element_type=F32) for w in w_refs]
    y = _silu(acc[0]) * acc[1] if n_w == 2 else acc[0]
    if has_resid:
        y = y + r_ref[...]
    o_ref[...] = y.astype(o_ref.dtype)


def _matmul(x, ws, *, tn, out_dtype, gain=None, resid=None, layer=None, name):
    m = x.shape[0]
    k, n = ws[0].shape[-2:]
    tm = ROW_TILE
    assert m % tm == 0 and n % tn == 0 and k % LANE == 0 and tm % NORM_ROWS == 0
    stage = gain is not None or x.dtype != BF16
    in_specs = [pl.BlockSpec((tm, k), lambda i, j: (i, 0), pipeline_mode=pl.Buffered(1))]
    args = [x]
    if gain is not None:
        in_specs.append(pl.BlockSpec((1, k), lambda i, j: (0, 0)))
        args.append(gain.reshape(1, k).astype(F32))
    for w in ws:
        if layer is None:
            in_specs.append(pl.BlockSpec((k, tn), lambda i, j: (0, j)))
        else:
            in_specs.append(pl.BlockSpec((None, k, tn), lambda i, j: (layer, 0, j)))
        args.append(w)
    if resid is not None:
        in_specs.append(pl.BlockSpec((tm, tn), lambda i, j: (i, j)))
        args.append(resid)
    scratch = [pltpu.VMEM((tm, k), BF16)] if stage else []
    return pl.pallas_call(
        functools.partial(_mm_kernel, has_norm=gain is not None, has_resid=resid is not None,
                          n_w=len(ws), stage=stage),
        out_shape=jax.ShapeDtypeStruct((m, n), out_dtype),
        grid=(m // tm, n // tn),
        in_specs=in_specs,
        out_specs=pl.BlockSpec((tm, tn), lambda i, j: (i, j)),
        scratch_shapes=scratch,
        compiler_params=_cparams(("parallel", "arbitrary")),
        name=name,
    )(*args)


def _rmsnorm_kernel(x_ref, g_ref, o_ref):
    x = x_ref[...]
    o_ref[...] = x * lax.rsqrt(jnp.mean(x * x, axis=-1, keepdims=True) + RMS_EPS) * g_ref[...]


def _rmsnorm_rows(x, gain, *, rows=208):
    m, d = x.shape
    assert m % rows == 0
    return pl.pallas_call(
        _rmsnorm_kernel,
        out_shape=jax.ShapeDtypeStruct((m, d), F32),
        grid=(m // rows,),
        in_specs=[pl.BlockSpec((rows, d), lambda i: (i, 0)), pl.BlockSpec((1, d), lambda i: (0, 0))],
        out_specs=pl.BlockSpec((rows, d), lambda i: (i, 0)),
        compiler_params=_cparams(("parallel",)),
        name="final_rmsnorm",
    )(x, gain.reshape(1, d))


def _gla_gate(sm, wg_ref, bg_ref):
    gp = jnp.dot(sm.astype(BF16), wg_ref[...].astype(BF16), preferred_element_type=F32) + bg_ref[...]
    return _log_sigmoid(gp) / GLA_GATE_TAU


def _gated_head_norm(o, nw_ref, gate):
    on = o * lax.rsqrt(jnp.mean(o * o, axis=-1, keepdims=True) + RMS_EPS) * nw_ref[...]
    return on * _silu(gate)


def _gla_prompt_kernel(q_ref, k_ref, v_ref, go_ref, sm_ref, wg_ref, bg_ref, nw_ref,
                       o_ref, st_ref, sT_ref):
    c = pl.program_id(2)
    C, dk = q_ref.shape
    sub = GLA_SUB

    @pl.when(c == 0)
    def _():
        sT_ref[...] = jnp.zeros_like(sT_ref)

    q = q_ref[...] * (dk ** -0.5)
    k = k_ref[...]
    v = v_ref[...].astype(BF16)
    g = _gla_gate(sm_ref[...], wg_ref, bg_ref)
    r_i = lax.broadcasted_iota(jnp.int32, (C, C), 0)
    c_i = lax.broadcasted_iota(jnp.int32, (C, C), 1)
    cum = _dotx((r_i >= c_i).astype(F32), g)

    key_row = lax.broadcasted_iota(jnp.int32, (C, 1), 0)
    lane = lax.broadcasted_iota(jnp.int32, (sub, C), 1)
    rloc = lax.broadcasted_iota(jnp.int32, (sub, C), 0)
    blocks = []
    for i in range(C // sub):
        r0 = i * sub
        qi = q[r0:r0 + sub]
        cumi = cum[r0:r0 + sub]
        dg = jnp.zeros((sub, C), F32)
        for s in range(r0, r0 + sub):
            x = qi * k[s:s + 1] * jnp.exp(jnp.minimum(cumi - cum[s:s + 1], 0.0))
            dg = jnp.where(lane == s, jnp.sum(x, axis=1, keepdims=True), dg)
        att_i = jnp.where(rloc + r0 >= lane, dg, 0.0)
        if i > 0:
            ci = cum[r0 - 1:r0]
            qt = qi * jnp.exp(cumi - ci)
            kt = jnp.where(key_row < r0, k * jnp.exp(jnp.minimum(ci - cum, 0.0)), 0.0)
            att_i = att_i + _nt(qt.astype(BF16), kt.astype(BF16))
        blocks.append(att_i)
    att = jnp.concatenate(blocks, axis=0)

    sT = sT_ref[...]
    last = cum[C - 1:C]
    o = jnp.dot(att.astype(BF16), v, preferred_element_type=F32)
    o = o + _nt((q * jnp.exp(cum)).astype(BF16), sT.astype(BF16))
    kd = k * jnp.exp(last - cum)
    sT_new = sT * jnp.exp(last) + _tn(v, kd.astype(BF16))
    sT_ref[...] = sT_new
    o_ref[...] = _gated_head_norm(o, nw_ref, go_ref[...]).astype(o_ref.dtype)

    @pl.when(c == pl.num_programs(2) - 1)
    def _():
        st_ref[0, 0] = sT_new.T


def _gla_prompt(proj, cols, wg_pad, b_gate, gla_norm, *, batch, seq, dk, dv):
    nc = seq // CHUNK
    assert seq % CHUNK == 0
    h = GLA_HEADS
    qb, kb, vb, gb, sb = (cols["gq"] // dk, cols["gk"] // dk, cols["gv"] // dv, cols["g_out"] // dv,
                          cols["small"] // LANE)
    row = lambda b, hh, c: b * nc + c
    in_specs = [
        pl.BlockSpec((CHUNK, dk), lambda b, hh, c: (row(b, hh, c), qb + hh)),
        pl.BlockSpec((CHUNK, dk), lambda b, hh, c: (row(b, hh, c), kb + hh)),
        pl.BlockSpec((CHUNK, dv), lambda b, hh, c: (row(b, hh, c), vb + hh)),
        pl.BlockSpec((CHUNK, dv), lambda b, hh, c: (row(b, hh, c), gb + hh)),
        pl.BlockSpec((CHUNK, LANE), lambda b, hh, c: (row(b, hh, c), sb)),
        pl.BlockSpec((LANE, dk), lambda b, hh, c: (0, hh)),
        pl.BlockSpec((1, dk), lambda b, hh, c: (0, hh)),
        pl.BlockSpec((1, dv), lambda b, hh, c: (0, 0)),
    ]
    return pl.pallas_call(
        _gla_prompt_kernel,
        out_shape=(jax.ShapeDtypeStruct((batch * seq, h * dv), BF16),
                   jax.ShapeDtypeStruct((batch, h, dk, dv), F32)),
        grid=(batch, h, nc),
        in_specs=in_specs,
        out_specs=(pl.BlockSpec((CHUNK, dv), lambda b, hh, c: (row(b, hh, c), hh)),
                   pl.BlockSpec((1, 1, dk, dv), lambda b, hh, c: (b, hh, 0, 0))),
        scratch_shapes=[pltpu.VMEM((dv, dk), F32)],
        compiler_params=_cparams(("parallel", "parallel", "arbitrary")),
        name="gla_prompt",
    )(proj, proj, proj, proj, proj, wg_pad, b_gate.reshape(1, -1), gla_norm.reshape(1, -1))


def _gla_sample_kernel(q_ref, k_ref, v_ref, go_ref, sm_ref, wg_ref, bg_ref, nw_ref, s_ref, *rest):
    o_ref, so_ref = rest[-2:]
    nb, dk = q_ref.shape
    qT = (q_ref[...] * (dk ** -0.5)).T
    kT = k_ref[...].T
    egT = jnp.exp(_gla_gate(sm_ref[...], wg_ref, bg_ref)).T
    v = v_ref[...]
    rows = []
    for b in range(nb):
        s_new = s_ref[b, 0] * egT[:, b:b + 1] + kT[:, b:b + 1] * v[b:b + 1]
        so_ref[b, 0] = s_new
        rows.append(jnp.sum(qT[:, b:b + 1] * s_new, axis=0, keepdims=True))
    o = jnp.concatenate(rows, axis=0)
    o_ref[...] = _gated_head_norm(o, nw_ref, go_ref[...]).astype(o_ref.dtype)


def _gla_sample(proj, cols, wg_pad, b_gate, gla_norm, states, j, prev, *, row0, dk, dv):
    nb = SAMPLE_ROWS
    bs, h = states.shape[1:3]
    assert bs % nb == 0 and row0 % nb == 0
    r0 = row0 // nb
    qb, kb, vb, gb, sb = (cols["gq"] // dk, cols["gk"] // dk, cols["gv"] // dv, cols["g_out"] // dv,
                          cols["small"] // LANE)
    in_specs = [
        pl.BlockSpec((nb, dk), lambda i, hh: (r0 + i, qb + hh)),
        pl.BlockSpec((nb, dk), lambda i, hh: (r0 + i, kb + hh)),
        pl.BlockSpec((nb, dv), lambda i, hh: (r0 + i, vb + hh)),
        pl.BlockSpec((nb, dv), lambda i, hh: (r0 + i, gb + hh)),
        pl.BlockSpec((nb, LANE), lambda i, hh: (r0 + i, sb)),
        pl.BlockSpec((LANE, dk), lambda i, hh: (0, hh)),
        pl.BlockSpec((1, dk), lambda i, hh: (0, hh)),
        pl.BlockSpec((1, dv), lambda i, hh: (0, 0)),
        pl.BlockSpec((None, nb, 1, dk, dv), lambda i, hh: (j, i, hh, 0, 0)),
    ]
    args = [proj, proj, proj, proj, proj, wg_pad, b_gate.reshape(1, -1), gla_norm.reshape(1, -1), states]
    aliases = {}
    if prev is not None:
        aliases = {len(args): 1}
        in_specs.append(pl.BlockSpec(memory_space=pl.ANY))
        args.append(prev)
    return pl.pallas_call(
        _gla_sample_kernel,
        out_shape=(jax.ShapeDtypeStruct((bs, h * dv), BF16),
                   jax.ShapeDtypeStruct(states.shape, F32)),
        grid=(bs // nb, h),
        in_specs=in_specs,
        out_specs=(pl.BlockSpec((nb, dv), lambda i, hh: (i, hh)),
                   pl.BlockSpec((None, nb, 1, dk, dv), lambda i, hh: (j, i, hh, 0, 0))),
        input_output_aliases=aliases,
        compiler_params=_cparams(("parallel", "parallel")),
        name="gla_sample",
    )(*args)


def _dn_gates(sm, alog_ref, dtb_ref):
    decay = -jnp.exp(alog_ref[...]) * _softplus(sm + dtb_ref[...])
    return decay, jax.nn.sigmoid(sm)


def _lane_col(x, lane_idx):
    lane = lax.broadcasted_iota(jnp.int32, x.shape, 1)
    return jnp.sum(jnp.where(lane == lane_idx, x, 0.0), axis=1, keepdims=True)


def _l2norm(x):
    return x * lax.rsqrt(jnp.sum(x * x, axis=-1, keepdims=True) + L2_EPS)


def _split_bf16(x):
    hi = x.astype(BF16)
    return hi, (x - hi.astype(F32)).astype(BF16)


def _bmm3(a, b):
    ah, al = _split_bf16(a)
    bh, bl = _split_bf16(b)
    mm = lambda x, y: jnp.einsum('gij,gjk->gik', x, y, preferred_element_type=F32)
    return mm(ah, bh) + (mm(ah, bl) + mm(al, bh))


def _unit_lower_inverse(a, sub):
    C = a.shape[-1]
    assert C // sub == 4
    r = lax.broadcasted_iota(jnp.int32, (C, C), 0)
    c = lax.broadcasted_iota(jnp.int32, (C, C), 1)
    eye = (r == c).astype(F32)
    same_blk = (r // sub) == (c // sub)
    x = jnp.where(same_blk, -a, 0.0)
    low = jnp.where(same_blk, 0.0, a)
    td = eye + x
    p = x
    for _ in range(int(math.log2(sub)) - 1):
        p = _bmm3(p, p)
        td = _bmm3(td, eye + p)
    m = _bmm3(td, low)
    tm = _bmm3(eye - m, eye + _bmm3(m, m))
    return _bmm3(tm, td)


def _heads(x, n):
    return jnp.stack([x[:, h * DN_HEAD:(h + 1) * DN_HEAD] for h in range(n)], axis=0)


def _unheads(x):
    return jnp.concatenate([x[h] for h in range(x.shape[0])], axis=1)


def _dn_prep_kernel(q_ref, k_ref, v_ref, sm_ref, wq_ref, wk_ref, wv_ref, alog_ref, dtb_ref,
                    u_ref, w_ref, qe_ref, kd_ref, att_ref, cum_ref, cq_ref, ck_ref, cv_ref, prev_ref,
                    *, a_lane, b_lane):
    hg = pl.program_id(1)
    c = pl.program_id(2)
    C = q_ref.shape[0]
    d = DN_HEAD
    G = q_ref.shape[1] // d

    @pl.when(c == 0)
    def _():
        prev_ref[...] = jnp.zeros_like(prev_ref)

    def conv(x_ref, w_ref_, slot):
        x = x_ref[...]
        ext = jnp.concatenate([prev_ref[slot], x], axis=0)
        y = x * w_ref_[CONV_WIDTH - 1:CONV_WIDTH]
        for i in range(1, CONV_WIDTH):
            y = y + pltpu.roll(ext, i, 0)[8:8 + C] * w_ref_[CONV_WIDTH - 1 - i:CONV_WIDTH - i]
        prev_ref[slot] = x[C - 8:C]
        return _silu(y)

    q3 = _heads(conv(q_ref, wq_ref, 0), G)
    k3 = _heads(conv(k_ref, wk_ref, 1), G)
    v2 = _heads(conv(v_ref, wv_ref, 2), G)
    q2 = _l2norm(q3) * (d ** -0.5)
    k2 = _l2norm(k3)

    decay, beta_all = _dn_gates(sm_ref[...], alog_ref, dtb_ref)
    r_i = lax.broadcasted_iota(jnp.int32, (C, C), 0)
    c_i = lax.broadcasted_iota(jnp.int32, (C, C), 1)
    cum_all = _dotx((r_i >= c_i).astype(F32), decay)
    cum_ref[...] = cum_all
    r_w = lax.broadcasted_iota(jnp.int32, (C, LANE), 0)
    c_w = lax.broadcasted_iota(jnp.int32, (C, LANE), 1)
    cums, betas, cum_rows = [], [], []
    for hl in range(G):
        head = hg * G + hl
        cum_h = _lane_col(cum_all, a_lane + head)
        cums.append(cum_h)
        betas.append(_lane_col(beta_all, b_lane + head))
        cum_rows.append(jnp.sum(jnp.where(r_w == c_w, cum_h, 0.0), axis=0, keepdims=True))
    cum = jnp.stack(cums, axis=0)
    beta = jnp.stack(betas, axis=0)
    cum_row = jnp.stack(cum_rows, axis=0)
    dec_w = jnp.exp(jnp.minimum(cum - cum_row, 0.0))
    dec = dec_w[:, :, :C]

    kbeta = k2 * beta
    k2b = k2.astype(BF16)
    kk = jnp.einsum('gtd,gsd->gts', kbeta.astype(BF16), k2b, preferred_element_type=F32)
    t = _unit_lower_inverse(jnp.where(r_i > c_i, kk * dec, 0.0), GLA_SUB)
    ecum = jnp.exp(cum)
    sol = _bmm3(t, jnp.concatenate([v2 * beta, kbeta * ecum], axis=-1))
    k_wide = jnp.concatenate([k2b, jnp.zeros_like(k2b)], axis=1)
    qk = jnp.einsum('gtd,gsd->gts', q2.astype(BF16), k_wide, preferred_element_type=F32)
    att = jnp.where(r_w >= c_w, qk * dec_w, 0.0)
    last = cum[:, C - 1:C]

    u_ref[...] = _unheads(sol[:, :, :d])
    w_ref[...] = _unheads(sol[:, :, d:]).astype(BF16)
    qe_ref[...] = _unheads(q2 * ecum).astype(BF16)
    kd_ref[...] = _unheads(k2 * jnp.exp(last - cum)).astype(BF16)
    att_ref[...] = _unheads(att).astype(BF16)

    @pl.when(c == pl.num_programs(2) - 1)
    def _():
        cq_ref[0] = prev_ref[0]
        ck_ref[0] = prev_ref[1]
        cv_ref[0] = prev_ref[2]


def _dn_scan_kernel(u_ref, w_ref, qe_ref, kd_ref, att_ref, cum_ref, z_ref, nw_ref, o_ref, st_ref, s_ref,
                    *, a_lane, group):
    c = pl.program_id(1)
    C = u_ref.shape[0]
    H = DN_HEADS

    @pl.when(c == 0)
    def _():
        s_ref[...] = jnp.zeros_like(s_ref)

    s = s_ref[...]
    sb = s.astype(BF16)
    v_new = _heads(u_ref[...], H) - jnp.einsum('hcd,hdv->hcv', _heads(w_ref[...], H), sb,
                                               preferred_element_type=F32)
    vb = v_new.astype(BF16)
    o = jnp.einsum('hts,hsv->htv', _heads(att_ref[...], H)[:, :, :C], vb, preferred_element_type=F32)
    o = o + jnp.einsum('hcd,hdv->hcv', _heads(qe_ref[...], H), sb, preferred_element_type=F32)
    cum_last = cum_ref[C - 1:C, :]
    lanes = [(h // group) * LANE + a_lane + h for h in range(H)]
    e_last = jnp.exp(jnp.stack([cum_last[:, l:l + 1] for l in lanes], axis=0))
    s_new = s * e_last + jnp.einsum('hcd,hcv->hdv', _heads(kd_ref[...], H), vb, preferred_element_type=F32)
    s_ref[...] = s_new
    on = o * lax.rsqrt(jnp.mean(o * o, axis=-1, keepdims=True) + RMS_EPS) * nw_ref[...]
    o_ref[...] = _unheads(on * _silu(_heads(z_ref[...], H))).astype(o_ref.dtype)

    @pl.when(c == pl.num_programs(1) - 1)
    def _():
        st_ref[0] = s_new


def _dn_prompt(proj, cols, conv_w, j, alog_pad, dtb_pad, dn_norm, *, batch, seq):
    nc = seq // CHUNK
    assert seq % CHUNK == 0
    G, d = DN_GROUP, DN_HEAD
    gw = G * d
    ng = DN_HEADS // G
    dn_k = DN_HEADS * d
    t = batch * seq
    qb, kb, vb, sb = cols["dn_q"] // gw, cols["dn_k"] // gw, cols["dn_v"] // gw, cols["small"] // LANE
    row = lambda b, g, c: b * nc + c
    wide = lambda dt: jax.ShapeDtypeStruct((t, dn_k), dt)
    tok = pl.BlockSpec((CHUNK, gw), lambda b, g, c: (row(b, g, c), g))
    tail = pl.BlockSpec((1, 8, gw), lambda b, g, c: (b, 0, g))
    u, w, qe, kd, att, cum, cq, ck, cv = pl.pallas_call(
        functools.partial(_dn_prep_kernel, a_lane=cols["a_lane"], b_lane=cols["b_lane"]),
        out_shape=(wide(F32), wide(BF16), wide(BF16), wide(BF16), wide(BF16),
                   jax.ShapeDtypeStruct((t, ng * LANE), F32),
                   jax.ShapeDtypeStruct((batch, 8, dn_k), F32), jax.ShapeDtypeStruct((batch, 8, dn_k), F32),
                   jax.ShapeDtypeStruct((batch, 8, dn_k), F32)),
        grid=(batch, ng, nc),
        in_specs=[
            pl.BlockSpec((CHUNK, gw), lambda b, g, c: (row(b, g, c), qb + g)),
            pl.BlockSpec((CHUNK, gw), lambda b, g, c: (row(b, g, c), kb + g)),
            pl.BlockSpec((CHUNK, gw), lambda b, g, c: (row(b, g, c), vb + g)),
            pl.BlockSpec((CHUNK, LANE), lambda b, g, c: (row(b, g, c), sb)),
            pl.BlockSpec((None, CONV_WIDTH, gw), lambda b, g, c: (j, 0, g)),
            pl.BlockSpec((None, CONV_WIDTH, gw), lambda b, g, c: (j, 0, dn_k // gw + g)),
            pl.BlockSpec((None, CONV_WIDTH, gw), lambda b, g, c: (j, 0, 2 * dn_k // gw + g)),
            pl.BlockSpec((1, LANE), lambda b, g, c: (0, 0)),
            pl.BlockSpec((1, LANE), lambda b, g, c: (0, 0)),
        ],
        out_specs=(tok, tok, tok, tok, tok,
                   pl.BlockSpec((CHUNK, LANE), lambda b, g, c: (row(b, g, c), g)), tail, tail, tail),
        scratch_shapes=[pltpu.VMEM((3, 8, gw), F32)],
        compiler_params=_cparams(("parallel", "parallel", "arbitrary")),
        name="deltanet_prep",
    )(proj, proj, proj, proj, conv_w, conv_w, conv_w, alog_pad, dtb_pad)

    full = lambda width: pl.BlockSpec((CHUNK, width), lambda b, c: (b * nc + c, 0))
    o, st = pl.pallas_call(
        functools.partial(_dn_scan_kernel, a_lane=cols["a_lane"], group=G),
        out_shape=(jax.ShapeDtypeStruct((t, dn_k), BF16),
                   jax.ShapeDtypeStruct((batch, DN_HEADS, d, d), F32)),
        grid=(batch, nc),
        in_specs=[full(dn_k), full(dn_k), full(dn_k), full(dn_k), full(dn_k), full(ng * LANE),
                  pl.BlockSpec((CHUNK, dn_k), lambda b, c: (b * nc + c, cols["dn_z"] // dn_k)),
                  pl.BlockSpec((1, d), lambda b, c: (0, 0))],
        out_specs=(full(dn_k), pl.BlockSpec((1, DN_HEADS, d, d), lambda b, c: (b, 0, 0, 0))),
        scratch_shapes=[pltpu.VMEM((DN_HEADS, d, d), F32)],
        compiler_params=_cparams(("parallel", "arbitrary")),
        name="deltanet_scan",
    )(u, w, qe, kd, att, cum, proj, dn_norm.reshape(1, -1))
    tail_rows = slice(8 - (CONV_WIDTH - 1), 8)
    conv_new = jnp.concatenate([cq[:, tail_rows], ck[:, tail_rows], cv[:, tail_rows]], axis=-1)
    return o, st, conv_new


def _dn_sample_kernel(q_ref, k_ref, v_ref, z_ref, sm_ref, bq_ref, bk_ref, bv_ref, wq_ref, wk_ref, wv_ref,
                      alog_ref, dtb_ref, nw_ref, s_ref, *rest, a_lane, b_lane):
    o_ref, so_ref = rest[-2:]
    hg = pl.program_id(1)
    nb = q_ref.shape[0]
    G = s_ref.shape[1]
    d = DN_HEAD

    def conv(x_ref, buf_ref, w_ref):
        y = x_ref[...] * w_ref[CONV_WIDTH - 1:CONV_WIDTH]
        for i in range(CONV_WIDTH - 1):
            y = y + buf_ref[:, i, :] * w_ref[i:i + 1]
        return _silu(y)

    yq = conv(q_ref, bq_ref, wq_ref)
    yk = conv(k_ref, bk_ref, wk_ref)
    yv = conv(v_ref, bv_ref, wv_ref)
    decay, beta_all = _dn_gates(sm_ref[...], alog_ref, dtb_ref)
    z = z_ref[...]
    outs = []
    for hl in range(G):
        head = hg * G + hl
        sl = slice(hl * d, (hl + 1) * d)
        q2 = _l2norm(yq[:, sl]) * (d ** -0.5)
        k2 = _l2norm(yk[:, sl])
        v2 = yv[:, sl]
        eg = jnp.exp(_lane_col(decay, a_lane + head))
        beta = _lane_col(beta_all, b_lane + head)
        qT = q2.T
        kT = k2.T
        rows = []
        for b in range(nb):
            s = s_ref[b, hl]
            kc = kT[:, b:b + 1]
            ks = jnp.sum(kc * s, axis=0, keepdims=True)
            v_new = beta[b:b + 1] * (v2[b:b + 1] - eg[b:b + 1] * ks)
            s_new = s * eg[b:b + 1] + kc * v_new
            so_ref[b, hl] = s_new
            rows.append(jnp.sum(qT[:, b:b + 1] * s_new, axis=0, keepdims=True))
        outs.append(_gated_head_norm(jnp.concatenate(rows, axis=0), nw_ref, z[:, sl]))
    o_ref[...] = jnp.concatenate(outs, axis=1).astype(o_ref.dtype)


def _dn_sample(proj, cols, conv_w, j, alog_pad, dtb_pad, dn_norm, states, conv_bufs, prev, *, row0):
    nb = SAMPLE_ROWS
    bs = states.shape[1]
    G, d = DN_GROUP, DN_HEAD
    gw = G * d
    ng = DN_HEADS // G
    dn_k = DN_HEADS * d
    assert bs % nb == 0 and row0 % nb == 0
    r0 = row0 // nb
    qb, kb, vb, zb, sb = (cols["dn_q"] // gw, cols["dn_k"] // gw, cols["dn_v"] // gw, cols["dn_z"] // gw,
                          cols["small"] // LANE)
    nbuf = CONV_WIDTH - 1
    in_specs = [
        pl.BlockSpec((nb, gw), lambda i, g: (r0 + i, qb + g)),
        pl.BlockSpec((nb, gw), lambda i, g: (r0 + i, kb + g)),
        pl.BlockSpec((nb, gw), lambda i, g: (r0 + i, vb + g)),
        pl.BlockSpec((nb, gw), lambda i, g: (r0 + i, zb + g)),
        pl.BlockSpec((nb, LANE), lambda i, g: (r0 + i, sb)),
        pl.BlockSpec((None, nb, nbuf, gw), lambda i, g: (j, i, 0, g)),
        pl.BlockSpec((None, nb, nbuf, gw), lambda i, g: (j, i, 0, dn_k // gw + g)),
        pl.BlockSpec((None, nb, nbuf, gw), lambda i, g: (j, i, 0, 2 * dn_k // gw + g)),
        pl.BlockSpec((None, CONV_WIDTH, gw), lambda i, g: (j, 0, g)),
        pl.BlockSpec((None, CONV_WIDTH, gw), lambda i, g: (j, 0, dn_k // gw + g)),
        pl.BlockSpec((None, CONV_WIDTH, gw), lambda i, g: (j, 0, 2 * dn_k // gw + g)),
        pl.BlockSpec((1, LANE), lambda i, g: (0, 0)),
        pl.BlockSpec((1, LANE), lambda i, g: (0, 0)),
        pl.BlockSpec((1, d), lambda i, g: (0, 0)),
        pl.BlockSpec((None, nb, G, d, d), lambda i, g: (j, i, g, 0, 0)),
    ]
    args = [proj, proj, proj, proj, proj, conv_bufs, conv_bufs, conv_bufs, conv_w, conv_w, conv_w,
            alog_pad, dtb_pad, dn_norm.reshape(1, -1), states]
    aliases = {}
    if prev is not None:
        aliases = {len(args): 1}
        in_specs.append(pl.BlockSpec(memory_space=pl.ANY))
        args.append(prev)
    return pl.pallas_call(
        functools.partial(_dn_sample_kernel, a_lane=cols["a_lane"], b_lane=cols["b_lane"]),
        out_shape=(jax.ShapeDtypeStruct((bs, DN_HEADS * d), BF16),
                   jax.ShapeDtypeStruct(states.shape, F32)),
        grid=(bs // nb, ng),
        in_specs=in_specs,
        out_specs=(pl.BlockSpec((nb, gw), lambda i, g: (i, g)),
                   pl.BlockSpec((None, nb, G, d, d), lambda i, g: (j, i, g, 0, 0))),
        input_output_aliases=aliases,
        compiler_params=_cparams(("parallel", "parallel")),
        name="deltanet_sample",
    )(*args)


def _mla_post_kernel(ckv_ref, kpe_ref, cs_ref, g_ref, lat_ref, kr_ref):
    x = ckv_ref[...]
    lat_ref[...] = x * lax.rsqrt(jnp.mean(x * x, axis=-1, keepdims=True) + RMS_EPS) * g_ref[...]
    zz = kpe_ref[...] * cs_ref[...]
    kr_ref[...] = zz + pltpu.roll(zz, QK_ROPE, 1)


def _mla_post(a, cs, kv_norm, *, rows=520):
    m = a.shape[0]
    assert m % rows == 0
    ckv_blk = Q_LORA // KV_LORA
    kpe_blk = (Q_LORA + KV_LORA) // LANE
    return pl.pallas_call(
        _mla_post_kernel,
        out_shape=(jax.ShapeDtypeStruct((m, KV_LORA), F32), jax.ShapeDtypeStruct((m, LANE), F32)),
        grid=(m // rows,),
        in_specs=[pl.BlockSpec((rows, KV_LORA), lambda i: (i, ckv_blk)),
                  pl.BlockSpec((rows, LANE), lambda i: (i, kpe_blk)),
                  pl.BlockSpec((rows, LANE), lambda i: (i, 0)),
                  pl.BlockSpec((1, KV_LORA), lambda i: (0, 0))],
        out_specs=(pl.BlockSpec((rows, KV_LORA), lambda i: (i, 0)),
                   pl.BlockSpec((rows, LANE), lambda i: (i, 0))),
        compiler_params=_cparams(("parallel",)),
        name="mla_latent_rope",
    )(a, a, cs, kv_norm.reshape(1, -1))


def _mla_prefill_kernel(qn_ref, qr_ref, cs_ref, kn_ref, v_ref, kr_ref, o_ref, *, scale):
    qi = pl.program_id(2)
    tq = qn_ref.shape[0]
    tk = tq
    q = jnp.concatenate([qn_ref[...].astype(BF16), (qr_ref[...] * cs_ref[...]).astype(BF16)], axis=1)
    qloc = lax.broadcasted_iota(jnp.int32, (tq, tk), 0)
    kloc = lax.broadcasted_iota(jnp.int32, (tq, tk), 1)

    def step(kj, carry, diagonal):
        m, l, acc = carry
        ks = pl.ds(pl.multiple_of(kj * tk, tk), tk)
        kf = jnp.concatenate([kn_ref[ks, :], kr_ref[ks, :].astype(BF16)], axis=1)
        s = _nt(q, kf) * scale
        if diagonal:
            s = jnp.where(kloc <= qloc, s, -jnp.inf)
        m_new = jnp.maximum(m, jnp.max(s, axis=-1, keepdims=True))
        corr = jnp.exp(m - m_new)
        p = jnp.exp(s - m_new)
        l_new = l * corr + jnp.sum(p, axis=-1, keepdims=True)
        acc_new = acc * corr + jnp.dot(p.astype(BF16), v_ref[ks, :], preferred_element_type=F32)
        return m_new, l_new, acc_new

    init = (jnp.full((tq, 1), -jnp.inf, F32), jnp.zeros((tq, 1), F32), jnp.zeros((tq, V_HEAD), F32))
    carry = lax.fori_loop(0, qi, lambda kj, cr: step(kj, cr, False), init)
    m, l, acc = step(qi, carry, True)
    o_ref[...] = (acc / l).astype(o_ref.dtype)


def _mla_prefill(q, cs, kv, kr2, *, batch, seq):
    tq = ATT_TQ
    assert seq % tq == 0
    nq = seq // tq
    h = MLA_HEADS
    scale = (QK_NOPE + QK_ROPE) ** -0.5
    return pl.pallas_call(
        functools.partial(_mla_prefill_kernel, scale=scale),
        out_shape=jax.ShapeDtypeStruct((batch * seq, h * V_HEAD), BF16),
        grid=(batch, h, nq),
        in_specs=[
            pl.BlockSpec((tq, QK_NOPE), lambda b, hh, i: (b * nq + i, hh)),
            pl.BlockSpec((tq, LANE), lambda b, hh, i: (b * nq + i, h + hh)),
            pl.BlockSpec((tq, LANE), lambda b, hh, i: (b * nq + i, 0)),
            pl.BlockSpec((seq, QK_NOPE), lambda b, hh, i: (b, 2 * hh)),
            pl.BlockSpec((seq, V_HEAD), lambda b, hh, i: (b, 2 * hh + 1)),
            pl.BlockSpec((seq, LANE), lambda b, hh, i: (b, 0)),
        ],
        out_specs=pl.BlockSpec((tq, V_HEAD), lambda b, hh, i: (b * nq + i, hh)),
        compiler_params=_cparams(("parallel", "parallel", "arbitrary")),
        name="mla_prefill",
    )(q, q, cs, kv, kv, kr2)


def _absorb_q_kernel(x_ref, w_ref, o_ref):
    o_ref[...] = _nt(x_ref[...].astype(BF16), w_ref[...].astype(BF16)).astype(o_ref.dtype)


def _absorb_q(q, w_kvb, j, *, row0, rows):
    h = MLA_HEADS
    assert row0 % rows == 0
    return pl.pallas_call(
        _absorb_q_kernel,
        out_shape=jax.ShapeDtypeStruct((rows, h * KV_LORA), BF16),
        grid=(h,),
        in_specs=[pl.BlockSpec((rows, QK_NOPE), lambda hh: (row0 // rows, hh)),
                  pl.BlockSpec((None, KV_LORA, QK_NOPE), lambda hh: (j, 0, 2 * hh))],
        out_specs=pl.BlockSpec((rows, KV_LORA), lambda hh: (0, hh)),
        compiler_params=_cparams(("parallel",)),
        name="mla_absorb_q",
    )(q, w_kvb)


def _absorb_v_kernel(x_ref, w_ref, o_ref):
    o_ref[...] = jnp.dot(x_ref[...].astype(BF16), w_ref[...].astype(BF16),
                         preferred_element_type=F32).astype(o_ref.dtype)


def _absorb_v(o_lat, w_kvb, j):
    rows = o_lat.shape[0]
    h = MLA_HEADS
    return pl.pallas_call(
        _absorb_v_kernel,
        out_shape=jax.ShapeDtypeStruct((rows, h * V_HEAD), BF16),
        grid=(h,),
        in_specs=[pl.BlockSpec((rows, KV_LORA), lambda hh: (0, hh)),
                  pl.BlockSpec((None, KV_LORA, V_HEAD), lambda hh: (j, 0, 2 * hh + 1))],
        out_specs=pl.BlockSpec((rows, V_HEAD), lambda hh: (0, hh)),
        compiler_params=_cparams(("parallel",)),
        name="mla_absorb_v",
    )(o_lat, w_kvb)


def _mla_decode_kernel(pt_ref, ql_ref, qr_ref, cs_ref, cn_ref, rn_ref, *rest, scale, n_pages):
    lat_refs = rest[:n_pages]
    rope_refs = rest[n_pages:2 * n_pages]
    o_ref, m_ref, l_ref, acc_ref = rest[2 * n_pages:]
    g = pl.program_id(1)
    ql = ql_ref[0]
    zz = qr_ref[0] * cs_ref[...]

    @pl.when(g == 0)
    def _():
        s0 = (jnp.sum(ql.astype(F32) * cn_ref[0], axis=-1, keepdims=True)
              + jnp.sum(zz * rn_ref[0], axis=-1, keepdims=True)) * scale
        m_ref[...] = s0
        l_ref[...] = jnp.ones_like(l_ref)
        acc_ref[...] = jnp.broadcast_to(cn_ref[0], acc_ref.shape)

    qrot = (zz + pltpu.roll(zz, QK_ROPE, 1))[:, :QK_ROPE].astype(BF16)
    cbs = [lat_refs[i][0, 0].astype(BF16) for i in range(n_pages)]
    s = jnp.concatenate(
        [_nt(ql, cbs[i]) + jnp.dot(qrot, rope_refs[i][0, 0].astype(BF16), preferred_element_type=F32)
         for i in range(n_pages)], axis=1) * scale
    m = m_ref[...]
    m_new = jnp.maximum(m, jnp.max(s, axis=-1, keepdims=True))
    corr = jnp.exp(m - m_new)
    e = jnp.exp(s - m_new)
    l_ref[...] = l_ref[...] * corr + jnp.sum(e, axis=-1, keepdims=True)
    eb = e.astype(BF16)
    page = cbs[0].shape[0]
    pv = jnp.dot(eb[:, :page], cbs[0], preferred_element_type=F32)
    for i in range(1, n_pages):
        pv = pv + jnp.dot(eb[:, i * page:(i + 1) * page], cbs[i], preferred_element_type=F32)
    acc_ref[...] = acc_ref[...] * corr + pv
    m_ref[...] = m_new

    @pl.when(g == pl.num_programs(1) - 1)
    def _():
        o_ref[0] = acc_ref[...] / l_ref[...]


def _mla_decode(q_lat, q_rope, cs_row, lat_new, kr_new, cache_latent, cache_k_rope_t, page_table, layer):
    bs = q_lat.shape[0]
    h = MLA_HEADS
    n_log = page_table.shape[1]
    page = cache_latent.shape[2]
    npg = DEC_PAGES
    assert n_log % npg == 0
    scale = (QK_NOPE + QK_ROPE) ** -0.5

    def page_spec(rows, width, i):
        return pl.BlockSpec((1, 1, rows, width),
                            lambda b, g, pt: (layer, pt[b * n_log + g * npg + i], 0, 0))

    in_specs = [
        pl.BlockSpec((1, h, KV_LORA), lambda b, g, pt: (b, 0, 0)),
        pl.BlockSpec((1, h, LANE), lambda b, g, pt: (b, 0, 0)),
        pl.BlockSpec((1, LANE), lambda b, g, pt: (0, 0)),
        pl.BlockSpec((1, 1, KV_LORA), lambda b, g, pt: (b, 0, 0)),
        pl.BlockSpec((1, 1, LANE), lambda b, g, pt: (b, 0, 0)),
    ]
    in_specs += [page_spec(page, KV_LORA, i) for i in range(npg)]
    in_specs += [page_spec(QK_ROPE, page, i) for i in range(npg)]
    grid_spec = pltpu.PrefetchScalarGridSpec(
        num_scalar_prefetch=1,
        grid=(bs, n_log // npg),
        in_specs=in_specs,
        out_specs=pl.BlockSpec((1, h, KV_LORA), lambda b, g, pt: (b, 0, 0)),
        scratch_shapes=[pltpu.VMEM((h, 1), F32), pltpu.VMEM((h, 1), F32), pltpu.VMEM((h, KV_LORA), F32)],
    )
    return pl.pallas_call(
        functools.partial(_mla_decode_kernel, scale=scale, n_pages=npg),
        out_shape=jax.ShapeDtypeStruct((bs, h, KV_LORA), F32),
        grid_spec=grid_spec,
        compiler_params=_cparams(("parallel", "arbitrary")),
        name="mla_decode",
    )(page_table.reshape(-1), q_lat.reshape(bs, h, KV_LORA), q_rope.reshape(bs, h, LANE), cs_row,
      lat_new.reshape(bs, 1, KV_LORA), kr_new.reshape(bs, 1, LANE),
      *([cache_latent] * npg), *([cache_k_rope_t] * npg))


def _rot_cols(w):
    half = QK_ROPE // 2
    return jnp.concatenate([-w[..., half:], w[..., :half]], axis=-1)


def _ab_weight_layout(w_all, j, d_model):
    gla_k = GLA_HEADS * (d_model // 16)
    gla_v = GLA_HEADS * (d_model // 8)
    dn = DN_HEADS * DN_HEAD
    splits = (gla_k, gla_k, gla_v, GLA_GATE_RANK, gla_v, 3 * dn, dn, DN_HEADS, DN_HEADS)
    idx = [0] + [int(i) for i in np.cumsum(splits)]
    n_small = GLA_GATE_RANK + 2 * DN_HEADS
    main = gla_k * 2 + gla_v * 2 + 4 * dn
    total = -(-(main + LANE) // 512) * 512
    pad = jnp.zeros((w_all.shape[1], total - main - n_small), w_all.dtype)
    w = jnp.concatenate([w_all[j, :, :idx[3]], w_all[j, :, idx[4]:idx[7]], w_all[j, :, idx[3]:idx[4]],
                         w_all[j, :, idx[7]:], pad], axis=1)
    cols = dict(gq=0, gk=gla_k, gv=2 * gla_k, g_out=2 * gla_k + gla_v,
                dn_q=2 * gla_k + 2 * gla_v, dn_k=2 * gla_k + 2 * gla_v + dn,
                dn_v=2 * gla_k + 2 * gla_v + 2 * dn, dn_z=2 * gla_k + 2 * gla_v + 3 * dn,
                small=main, a_lane=GLA_GATE_RANK, b_lane=GLA_GATE_RANK + DN_HEADS)
    return w, cols


def _pad_lanes(v, lane0):
    return jnp.zeros((1, LANE), F32).at[0, lane0:lane0 + v.shape[0]].set(v.astype(F32))


def _rope_table(pos):
    half = QK_ROPE // 2
    inv = ROPE_THETA ** (-jnp.arange(half, dtype=F32) / half)
    ang = pos.astype(F32)[:, None] * inv[None, :]
    cos, sin = jnp.cos(ang), jnp.sin(ang)
    return jnp.concatenate([cos, cos, sin, sin], axis=1)


def _mla_mixer(x, cs, norm_w, w_a, q_norm, kv_norm, w_qb, w_kvb, w_o, cache_latent, cache_k_rope_t,
               page_table, j, *, bp, lp, bs):
    tp = bp * lp
    d_model = x.shape[1]
    kpe_w = w_a[:, Q_LORA + KV_LORA:]
    a_pad = jnp.zeros((d_model, -(Q_LORA + KV_LORA + LANE) % 512), F32)
    w_a2 = jnp.concatenate([w_a[:, :Q_LORA + KV_LORA], kpe_w, _rot_cols(kpe_w), a_pad], axis=1)
    a = _matmul(x, [w_a2], tn=512, out_dtype=F32, gain=norm_w, name="mla_a_proj")
    latent, kr2 = _mla_post(a, cs, kv_norm)
    w_qb = w_qb.reshape(Q_LORA, MLA_HEADS, QK_NOPE + QK_ROPE)
    w_rope = w_qb[..., QK_NOPE:]
    w_q2 = jnp.concatenate(
        [w_qb[..., :QK_NOPE].reshape(Q_LORA, -1),
         jnp.concatenate([w_rope, _rot_cols(w_rope)], axis=-1).reshape(Q_LORA, -1)], axis=1)
    q = _matmul(a, [w_q2], tn=512, out_dtype=F32, gain=q_norm, name="mla_q_proj")
    kv = _matmul(latent, [w_kvb], tn=512, out_dtype=BF16, layer=j, name="mla_kv_proj")
    o_p = _mla_prefill(q, cs, kv, kr2, batch=bp, seq=lp)
    q_lat = _absorb_q(q, w_kvb, j, row0=tp, rows=bs)
    q_rope_s = q[tp:, MLA_HEADS * QK_NOPE:]
    o_lat = _mla_decode(q_lat, q_rope_s, cs[tp:tp + 1], latent[tp:], kr2[tp:],
                        cache_latent, cache_k_rope_t, page_table, j)
    o_s = _absorb_v(o_lat.reshape(bs, MLA_HEADS * KV_LORA), w_kvb, j)
    att = jnp.concatenate([o_p, o_s], axis=0)
    x = _matmul(att, [w_o], tn=512, out_dtype=F32, resid=x, layer=j, name="mla_out_proj")
    return x, latent, kr2


def kernel(x_prompt, x_sample, state_gla, state_delta, state_conv, cache_latent, cache_k_rope, page_table,
           norm_mix, norm_ffn, norm_final, ab_w_in, gla_w_gate, gla_b_gate, gla_norm, dn_conv_w,
           dn_a_log, dn_dt_bias, dn_norm, ab_w_out, mla_w_a, mla_q_norm, mla_kv_norm, mla_w_qb,
           mla_w_kvb, mla_w_o, ffn_w_gate, ffn_w_up, ffn_w_down):
    bp, lp, d_model = x_prompt.shape
    bs, ls, _ = x_sample.shape
    assert ls == 1
    tp = bp * lp
    depth = norm_mix.shape[0]
    past_len = page_table.shape[1] * cache_latent.shape[2]
    gla_dk, gla_dv = state_gla.shape[3:]
    dn_ch = state_conv.shape[-1]

    x = jnp.concatenate([x_prompt.reshape(tp, d_model), x_sample.reshape(bs, d_model)], axis=0)
    pos = jnp.concatenate([jnp.tile(jnp.arange(lp), bp), jnp.full((bs,), past_len)])
    cs = _rope_table(pos)

    cache_k_rope_t = jnp.swapaxes(cache_k_rope, 2, 3)
    gla_p, dn_p, cv_p, cv_s = [], [], [], []
    lat_p, lat_s, kr_p, kr_s = [], [], [], []
    gla_s = dn_s = None
    for layer in range(depth):
        j = layer // 2
        if layer % 2 == 0:
            w_in, cols = _ab_weight_layout(ab_w_in, j, d_model)
            proj = _matmul(x, [w_in], tn=512, out_dtype=F32, gain=norm_mix[layer], name="ab_in_proj")
            wg_pad = jnp.zeros((LANE, gla_w_gate.shape[2]), F32).at[:GLA_GATE_RANK].set(gla_w_gate[j])
            alog_pad = _pad_lanes(dn_a_log[j], cols["a_lane"])
            dtb_pad = _pad_lanes(dn_dt_bias[j], cols["a_lane"])
            og_p, g1 = _gla_prompt(proj, cols, wg_pad, gla_b_gate[j], gla_norm[j],
                                   batch=bp, seq=lp, dk=gla_dk, dv=gla_dv)
            og_s, gla_s = _gla_sample(proj, cols, wg_pad, gla_b_gate[j], gla_norm[j], state_gla, j, gla_s,
                                      row0=tp, dk=gla_dk, dv=gla_dv)
            od_p, d1, c1 = _dn_prompt(proj, cols, dn_conv_w, j, alog_pad, dtb_pad, dn_norm[j], batch=bp, seq=lp)
            od_s, dn_s = _dn_sample(proj, cols, dn_conv_w, j, alog_pad, dtb_pad, dn_norm[j],
                                    state_delta, state_conv, dn_s, row0=tp)
            c2 = jnp.concatenate([state_conv[j, :, 1:], proj[tp:, None, cols["dn_q"]:cols["dn_q"] + dn_ch]], axis=1)
            mix = jnp.concatenate([jnp.concatenate([og_p, od_p], axis=1),
                                   jnp.concatenate([og_s, od_s], axis=1)], axis=0)
            x = _matmul(mix, [ab_w_out], tn=512, out_dtype=F32, resid=x, layer=j, name="ab_out_proj")
            gla_p.append(g1)
            dn_p.append(d1)
            cv_p.append(c1); cv_s.append(c2)
        else:
            x, latent, kr2 = _mla_mixer(x, cs, norm_mix[layer], mla_w_a[j], mla_q_norm[j], mla_kv_norm[j],
                                        mla_w_qb[j], mla_w_kvb, mla_w_o, cache_latent, cache_k_rope_t,
                                        page_table, j, bp=bp, lp=lp, bs=bs)
            lat_p.append(latent[:tp].reshape(bp, lp, KV_LORA)); lat_s.append(latent[tp:].reshape(bs, 1, KV_LORA))
            kr_p.append(kr2[:tp, :QK_ROPE].reshape(bp, lp, QK_ROPE)); kr_s.append(kr2[tp:, :QK_ROPE].reshape(bs, 1, QK_ROPE))
        hid = _matmul(x, [ffn_w_gate, ffn_w_up], tn=256, out_dtype=BF16, gain=norm_ffn[layer], layer=layer,
                      name="ffn_gate_up")
        x = _matmul(hid, [ffn_w_down], tn=256, out_dtype=F32, resid=x, layer=layer, name="ffn_down")
    y = _rmsnorm_rows(x, norm_final)
    y_prompt = y[:tp].reshape(bp, lp, d_model)
    y_sample = y[tp:].reshape(bs, 1, d_model)
    return (y_prompt, y_sample, jnp.stack(gla_p), gla_s, jnp.stack(dn_p), dn_s,
            jnp.stack(cv_p), jnp.stack(cv_s), jnp.stack(lat_p), jnp.stack(lat_s), jnp.stack(kr_p), jnp.stack(kr_s))
```

```python
import functools
import math

import numpy as np
import jax
import jax.numpy as jnp
from jax import lax
from jax.experimental import pallas as pl
from jax.experimental.pallas import tpu as pltpu

F32 = jnp.float32
BF16 = jnp.bfloat16
HIGHEST = lax.Precision.HIGHEST

GLA_HEADS = 4
GLA_GATE_RANK = 16
GLA_GATE_TAU = 16.0
DN_HEADS = 16
DN_HEAD = 128
CONV_WIDTH = 4
CHUNK = 64
MLA_HEADS = 32
Q_LORA = 1024
KV_LORA = 512
QK_NOPE = 128
QK_ROPE = 64
V_HEAD = 128
ROPE_THETA = 10000.0
RMS_EPS = 1e-6
L2_EPS = 1e-6

LANE = 128
VMEM_LIMIT_BYTES = 60 * 1024 * 1024

ROW_TILE = 1040
NORM_ROWS = 16
GLA_SUB = 16
GLA_STEP_CHUNKS = 4
DN_GROUP = 8
ATT_TQ = 512
DEC_PAGES = 16
SAMPLE_ROWS = 8


def _cparams(sem):
    return pltpu.CompilerParams(dimension_semantics=sem, vmem_limit_bytes=VMEM_LIMIT_BYTES)


def _silu(x):
    return x * jax.nn.sigmoid(x)


def _log_sigmoid(x):
    return jnp.minimum(x, 0.0) - jnp.log(1.0 + jnp.exp(-jnp.abs(x)))


def _softplus(x):
    return jnp.maximum(x, 0.0) + jnp.log(1.0 + jnp.exp(-jnp.abs(x)))


def _nt(a, b):
    return lax.dot_general(a, b, (((1,), (1,)), ((), ())), preferred_element_type=F32)


def _tn(a, b):
    return lax.dot_general(a, b, (((0,), (0,)), ((), ())), preferred_element_type=F32)


def _dotx(a, b):
    return jnp.dot(a, b, precision=HIGHEST, preferred_element_type=F32)


def _stage_lhs(x_ref, g_ref, xs_ref):
    @pl.when(pl.program_id(1) == 0)
    def _():
        def body(r, carry):
            sl = pl.ds(pl.multiple_of(r * NORM_ROWS, NORM_ROWS), NORM_ROWS)
            x = x_ref[sl, :].astype(F32)
            if g_ref is not None:
                x = x * lax.rsqrt(jnp.mean(x * x, axis=-1, keepdims=True) + RMS_EPS) * g_ref[...]
            xs_ref[sl, :] = x.astype(BF16)
            return carry
        lax.fori_loop(0, x_ref.shape[0] // NORM_ROWS, body, 0)


def _mm_kernel(*refs, has_norm, has_resid, n_w, stage):
    it = iter(refs)
    x_ref = next(it)
    g_ref = next(it) if has_norm else None
    w_refs = [next(it) for _ in range(n_w)]
    r_ref = next(it) if has_resid else None
    o_ref = next(it)
    xs_ref = next(it) if stage else None

    if stage:
        _stage_lhs(x_ref, g_ref, xs_ref)
        xb = xs_ref[...]
    else:
        xb = x_ref[...]
    acc = [jnp.dot(xb, w[...].astype(BF16), preferred_element_type=F32) for w in w_refs]
    y = _silu(acc[0]) * acc[1] if n_w == 2 else acc[0]
    if has_resid:
        y = y + r_ref[...]
    o_ref[...] = y.astype(o_ref.dtype)


def _matmul(x, ws, *, tn, out_dtype, gain=None, resid=None, layer=None, name):
    m = x.shape[0]
    k, n = ws[0].shape[-2:]
    tm = ROW_TILE
    assert m % tm == 0 and n % tn == 0 and k % LANE == 0 and tm % NORM_ROWS == 0
    stage = gain is not None or x.dtype != BF16
    in_specs = [pl.BlockSpec((tm, k), lambda i, j: (i, 0), pipeline_mode=pl.Buffered(1))]
    args = [x]
    if gain is not None:
        in_specs.append(pl.BlockSpec((1, k), lambda i, j: (0, 0)))
        args.append(gain.reshape(1, k).astype(F32))
    for w in ws:
        if layer is None:
            in_specs.append(pl.BlockSpec((k, tn), lambda i, j: (0, j)))
        else:
            in_specs.append(pl.BlockSpec((None, k, tn), lambda i, j: (layer, 0, j)))
        args.append(w)
    if resid is not None:
        in_specs.append(pl.BlockSpec((tm, tn), lambda i, j: (i, j)))
        args.append(resid)
    scratch = [pltpu.VMEM((tm, k), BF16)] if stage else []
    return pl.pallas_call(
        functools.partial(_mm_kernel, has_norm=gain is not None, has_resid=resid is not None,
                          n_w=len(ws), stage=stage),
        out_shape=jax.ShapeDtypeStruct((m, n), out_dtype),
        grid=(m // tm, n // tn),
        in_specs=in_specs,
        out_specs=pl.BlockSpec((tm, tn), lambda i, j: (i, j)),
        scratch_shapes=scratch,
        compiler_params=_cparams(("parallel", "arbitrary")),
        name=name,
    )(*args)


def _ab_in_kernel(x_ref, g_ref, w_ref, wx_ref, o_ref, xs_ref, *, n_plain, shift):
    j = pl.program_id(1)
    tn = o_ref.shape[1]
    _stage_lhs(x_ref, g_ref, xs_ref)

    @pl.when(j < n_plain)
    def _():
        o_ref[...] = jnp.dot(xs_ref[...], w_ref[...].astype(BF16), preferred_element_type=F32)

    @pl.when(j >= n_plain)
    def _():
        w = jnp.concatenate([w_ref[...], wx_ref[...]], axis=1).astype(BF16)[:, shift:shift + tn]
        o_ref[...] = jnp.dot(xs_ref[...], w, preferred_element_type=F32)


def _ab_in_proj(x, gain, w_all, layer, *, aligned, shift, n_out, tn=512):
    m, k = x.shape
    tm = ROW_TILE
    assert m % tm == 0 and aligned % tn == 0 and n_out % tn == 0 and 0 < shift < LANE
    n_plain = aligned // tn
    n_tiles = n_out // tn
    sub = tn // LANE
    return pl.pallas_call(
        functools.partial(_ab_in_kernel, n_plain=n_plain, shift=shift),
        out_shape=jax.ShapeDtypeStruct((m, n_out), F32),
        grid=(m // tm, n_tiles),
        in_specs=[pl.BlockSpec((tm, k), lambda i, j: (i, 0), pipeline_mode=pl.Buffered(1)),
                  pl.BlockSpec((1, k), lambda i, j: (0, 0)),
                  pl.BlockSpec((None, k, tn), lambda i, j: (layer, 0, j)),
                  pl.BlockSpec((None, k, LANE), lambda i, j: (layer, 0, sub * (jnp.maximum(j, n_plain) + 1)))],
        out_specs=pl.BlockSpec((tm, tn), lambda i, j: (i, j)),
        scratch_shapes=[pltpu.VMEM((tm, k), BF16)],
        compiler_params=_cparams(("parallel", "arbitrary")),
        name="ab_in_proj",
    )(x, gain.reshape(1, k).astype(F32), w_all, w_all)


def _rmsnorm_kernel(x_ref, g_ref, o_ref):
    x = x_ref[...]
    o_ref[...] = x * lax.rsqrt(jnp.mean(x * x, axis=-1, keepdims=True) + RMS_EPS) * g_ref[...]


def _rmsnorm_rows(x, gain, *, row0, n_rows, rows):
    d = x.shape[1]
    assert n_rows % rows == 0 and row0 % rows == 0
    return pl.pallas_call(
        _rmsnorm_kernel,
        out_shape=jax.ShapeDtypeStruct((n_rows, d), F32),
        grid=(n_rows // rows,),
        in_specs=[pl.BlockSpec((rows, d), lambda i: (row0 // rows + i, 0)), pl.BlockSpec((1, d), lambda i: (0, 0))],
        out_specs=pl.BlockSpec((rows, d), lambda i: (i, 0)),
        compiler_params=_cparams(("parallel",)),
        name="final_rmsnorm",
    )(x, gain.reshape(1, d))


def _gla_gate(sm, wg_ref, bg_ref):
    gp = jnp.dot(sm.astype(BF16), wg_ref[...].astype(BF16), preferred_element_type=F32) + bg_ref[...]
    return _log_sigmoid(gp) / GLA_GATE_TAU


def _gated_head_norm(o, nw_ref, gate):
    on = o * lax.rsqrt(jnp.mean(o * o, axis=-1, keepdims=True) + RMS_EPS) * nw_ref[...]
    return on * _silu(gate)


def _gla_prompt_kernel(q_ref, k_ref, v_ref, go_ref, sm_ref, wg_ref, bg_ref, nw_ref,
                       o_ref, st_ref, sT_ref):
    c = pl.program_id(2)
    dk = q_ref.shape[1]
    C = CHUNK
    sub = GLA_SUB

    @pl.when(c == 0)
    def _():
        sT_ref[...] = jnp.zeros_like(sT_ref)

    r_i = lax.broadcasted_iota(jnp.int32, (C, C), 0)
    c_i = lax.broadcasted_iota(jnp.int32, (C, C), 1)
    tril = (r_i >= c_i).astype(F32)
    key_row = lax.broadcasted_iota(jnp.int32, (C, 1), 0)
    lane = lax.broadcasted_iota(jnp.int32, (sub, C), 1)
    rloc = lax.broadcasted_iota(jnp.int32, (sub, C), 0)

    for cc in range(q_ref.shape[0] // C):
        rows = slice(cc * C, (cc + 1) * C)
        q = q_ref[rows, :] * (dk ** -0.5)
        k = k_ref[rows, :]
        v = v_ref[rows, :].astype(BF16)
        g = _gla_gate(sm_ref[rows, :], wg_ref, bg_ref)
        cum = _dotx(tril, g)
        blocks = []
        for i in range(C // sub):
            r0 = i * sub
            qi = q[r0:r0 + sub]
            cumi = cum[r0:r0 + sub]
            dg = jnp.zeros((sub, C), F32)
            for s in range(r0, r0 + sub):
                x = qi * k[s:s + 1] * jnp.exp(jnp.minimum(cumi - cum[s:s + 1], 0.0))
                dg = jnp.where(lane == s, jnp.sum(x, axis=1, keepdims=True), dg)
            att_i = jnp.where(rloc + r0 >= lane, dg, 0.0)
            if i > 0:
                ci = cum[r0 - 1:r0]
                qt = qi * jnp.exp(cumi - ci)
                kt = jnp.where(key_row < r0, k * jnp.exp(jnp.minimum(ci - cum, 0.0)), 0.0)
                att_i = att_i + _nt(qt.astype(BF16), kt.astype(BF16))
            blocks.append(att_i)
        att = jnp.concatenate(blocks, axis=0)

        sT = sT_ref[...]
        last = cum[C - 1:C]
        o = jnp.dot(att.astype(BF16), v, preferred_element_type=F32)
        o = o + _nt((q * jnp.exp(cum)).astype(BF16), sT.astype(BF16))
        kd = k * jnp.exp(last - cum)
        sT_ref[...] = sT * jnp.exp(last) + _tn(v, kd.astype(BF16))
        o_ref[rows, :] = _gated_head_norm(o, nw_ref, go_ref[rows, :]).astype(o_ref.dtype)

    @pl.when(c == pl.num_programs(2) - 1)
    def _():
        st_ref[0, 0] = sT_ref[...].T


def _gla_prompt(proj, small, cols, wg_pad, b_gate, gla_norm, *, batch, seq, dk, dv, mix_shape):
    step = GLA_STEP_CHUNKS * CHUNK
    nc = seq // step
    assert seq % step == 0
    h = GLA_HEADS
    qb, kb, vb, gb = cols["gq"] // dk, cols["gk"] // dk, cols["gv"] // dv, cols["g_out"] // dv
    row = lambda b, hh, c: b * nc + c
    in_specs = [
        pl.BlockSpec((step, dk), lambda b, hh, c: (row(b, hh, c), qb + hh)),
        pl.BlockSpec((step, dk), lambda b, hh, c: (row(b, hh, c), kb + hh)),
        pl.BlockSpec((step, dv), lambda b, hh, c: (row(b, hh, c), vb + hh)),
        pl.BlockSpec((step, dv), lambda b, hh, c: (row(b, hh, c), gb + hh)),
        pl.BlockSpec((step, LANE), lambda b, hh, c: (row(b, hh, c), 0)),
        pl.BlockSpec((LANE, dk), lambda b, hh, c: (0, hh)),
        pl.BlockSpec((1, dk), lambda b, hh, c: (0, hh)),
        pl.BlockSpec((1, dv), lambda b, hh, c: (0, 0)),
    ]
    return pl.pallas_call(
        _gla_prompt_kernel,
        out_shape=(jax.ShapeDtypeStruct(mix_shape, BF16),
                   jax.ShapeDtypeStruct((batch, h, dk, dv), F32)),
        grid=(batch, h, nc),
        in_specs=in_specs,
        out_specs=(pl.BlockSpec((step, dv), lambda b, hh, c: (row(b, hh, c), hh)),
                   pl.BlockSpec((1, 1, dk, dv), lambda b, hh, c: (b, hh, 0, 0))),
        scratch_shapes=[pltpu.VMEM((dv, dk), F32)],
        compiler_params=_cparams(("parallel", "parallel", "arbitrary")),
        name="gla_prompt",
    )(proj, proj, proj, proj, small, wg_pad, b_gate.reshape(1, -1), gla_norm.reshape(1, -1))


def _alias_outputs(in_specs, args, bufs):
    aliases = {}
    for k, buf in enumerate(bufs):
        if buf is not None:
            aliases[len(args)] = k
            in_specs.append(pl.BlockSpec(memory_space=pl.ANY))
            args.append(buf)
    return aliases


def _gla_sample_kernel(q_ref, k_ref, v_ref, go_ref, sm_ref, wg_ref, bg_ref, nw_ref, s_ref, *rest):
    o_ref, so_ref = rest[-2:]
    nb, dk = q_ref.shape
    qT = (q_ref[...] * (dk ** -0.5)).T
    kT = k_ref[...].T
    egT = jnp.exp(_gla_gate(sm_ref[...], wg_ref, bg_ref)).T
    v = v_ref[...]
    rows = []
    for b in range(nb):
        s_new = s_ref[b, 0] * egT[:, b:b + 1] + kT[:, b:b + 1] * v[b:b + 1]
        so_ref[b, 0] = s_new
        rows.append(jnp.sum(qT[:, b:b + 1] * s_new, axis=0, keepdims=True))
    o = jnp.concatenate(rows, axis=0)
    o_ref[...] = _gated_head_norm(o, nw_ref, go_ref[...]).astype(o_ref.dtype)


def _gla_sample(proj, small, cols, wg_pad, b_gate, gla_norm, states, j, prev, mix, *, row0, dk, dv):
    nb = SAMPLE_ROWS
    bs, h = states.shape[1:3]
    assert bs % nb == 0 and row0 % nb == 0
    r0 = row0 // nb
    qb, kb, vb, gb = cols["gq"] // dk, cols["gk"] // dk, cols["gv"] // dv, cols["g_out"] // dv
    in_specs = [
        pl.BlockSpec((nb, dk), lambda i, hh: (r0 + i, qb + hh)),
        pl.BlockSpec((nb, dk), lambda i, hh: (r0 + i, kb + hh)),
        pl.BlockSpec((nb, dv), lambda i, hh: (r0 + i, vb + hh)),
        pl.BlockSpec((nb, dv), lambda i, hh: (r0 + i, gb + hh)),
        pl.BlockSpec((nb, LANE), lambda i, hh: (r0 + i, 0)),
        pl.BlockSpec((LANE, dk), lambda i, hh: (0, hh)),
        pl.BlockSpec((1, dk), lambda i, hh: (0, hh)),
        pl.BlockSpec((1, dv), lambda i, hh: (0, 0)),
        pl.BlockSpec((None, nb, 1, dk, dv), lambda i, hh: (j, i, hh, 0, 0)),
    ]
    args = [proj, proj, proj, proj, small, wg_pad, b_gate.reshape(1, -1), gla_norm.reshape(1, -1), states]
    aliases = _alias_outputs(in_specs, args, [mix, prev])
    return pl.pallas_call(
        _gla_sample_kernel,
        out_shape=(jax.ShapeDtypeStruct(mix.shape, BF16),
                   jax.ShapeDtypeStruct(states.shape, F32)),
        grid=(bs // nb, h),
        in_specs=in_specs,
        out_specs=(pl.BlockSpec((nb, dv), lambda i, hh: (r0 + i, hh)),
                   pl.BlockSpec((None, nb, 1, dk, dv), lambda i, hh: (j, i, hh, 0, 0))),
        input_output_aliases=aliases,
        compiler_params=_cparams(("parallel", "parallel")),
        name="gla_sample",
    )(*args)


def _dn_gates(sm, alog_ref, dtb_ref):
    decay = -jnp.exp(alog_ref[...]) * _softplus(sm + dtb_ref[...])
    return decay, jax.nn.sigmoid(sm)


def _lane_col(x, lane_idx):
    lane = lax.broadcasted_iota(jnp.int32, x.shape, 1)
    return jnp.sum(jnp.where(lane == lane_idx, x, 0.0), axis=1, keepdims=True)


def _l2norm(x):
    return x * lax.rsqrt(jnp.sum(x * x, axis=-1, keepdims=True) + L2_EPS)


def _split_bf16(x):
    hi = x.astype(BF16)
    return hi, (x - hi.astype(F32)).astype(BF16)


def _bmm3(a, b):
    ah, al = _split_bf16(a)
    bh, bl = _split_bf16(b)
    mm = lambda x, y: jnp.einsum('gij,gjk->gik', x, y, preferred_element_type=F32)
    return mm(ah, bh) + (mm(ah, bl) + mm(al, bh))


def _unit_lower_inverse(a, sub):
    C = a.shape[-1]
    assert C // sub == 4
    r = lax.broadcasted_iota(jnp.int32, (C, C), 0)
    c = lax.broadcasted_iota(jnp.int32, (C, C), 1)
    eye = (r == c).astype(F32)
    same_blk = (r // sub) == (c // sub)
    x = jnp.where(same_blk, -a, 0.0)
    low = jnp.where(same_blk, 0.0, a)
    td = eye + x
    p = x
    for _ in range(int(math.log2(sub)) - 1):
        p = _bmm3(p, p)
        td = _bmm3(td, eye + p)
    m = _bmm3(td, low)
    tm = _bmm3(eye - m, eye + _bmm3(m, m))
    return _bmm3(tm, td)


def _heads(x, n):
    return jnp.stack([x[:, h * DN_HEAD:(h + 1) * DN_HEAD] for h in range(n)], axis=0)


def _unheads(x):
    return jnp.concatenate([x[h] for h in range(x.shape[0])], axis=1)


def _dn_prep_kernel(q_ref, k_ref, v_ref, sm_ref, wq_ref, wk_ref, wv_ref, alog_ref, dtb_ref,
                    u_ref, w_ref, qe_ref, kd_ref, att_ref, cum_ref, cq_ref, ck_ref, cv_ref, prev_ref,
                    *, a_lane, b_lane):
    hg = pl.program_id(1)
    c = pl.program_id(2)
    C = q_ref.shape[0]
    d = DN_HEAD
    G = q_ref.shape[1] // d

    @pl.when(c == 0)
    def _():
        prev_ref[...] = jnp.zeros_like(prev_ref)

    def conv(x_ref, w_ref_, slot):
        x = x_ref[...]
        ext = jnp.concatenate([prev_ref[slot], x], axis=0)
        y = x * w_ref_[CONV_WIDTH - 1:CONV_WIDTH]
        for i in range(1, CONV_WIDTH):
            y = y + pltpu.roll(ext, i, 0)[8:8 + C] * w_ref_[CONV_WIDTH - 1 - i:CONV_WIDTH - i]
        prev_ref[slot] = x[C - 8:C]
        return _silu(y)

    q3 = _heads(conv(q_ref, wq_ref, 0), G)
    k3 = _heads(conv(k_ref, wk_ref, 1), G)
    v2 = _heads(conv(v_ref, wv_ref, 2), G)
    q2 = _l2norm(q3) * (d ** -0.5)
    k2 = _l2norm(k3)

    decay, beta_all = _dn_gates(sm_ref[...], alog_ref, dtb_ref)
    r_i = lax.broadcasted_iota(jnp.int32, (C, C), 0)
    c_i = lax.broadcasted_iota(jnp.int32, (C, C), 1)
    cum_all = _dotx((r_i >= c_i).astype(F32), decay)
    cum_ref[...] = cum_all
    r_w = lax.broadcasted_iota(jnp.int32, (C, LANE), 0)
    c_w = lax.broadcasted_iota(jnp.int32, (C, LANE), 1)
    cums, betas, cum_rows = [], [], []
    for hl in range(G):
        head = hg * G + hl
        cum_h = _lane_col(cum_all, a_lane + head)
        cums.append(cum_h)
        betas.append(_lane_col(beta_all, b_lane + head))
        cum_rows.append(jnp.sum(jnp.where(r_w == c_w, cum_h, 0.0), axis=0, keepdims=True))
    cum = jnp.stack(cums, axis=0)
    beta = jnp.stack(betas, axis=0)
    cum_row = jnp.stack(cum_rows, axis=0)
    dec_w = jnp.exp(jnp.minimum(cum - cum_row, 0.0))
    dec = dec_w[:, :, :C]

    kbeta = k2 * beta
    k2b = k2.astype(BF16)
    kk = jnp.einsum('gtd,gsd->gts', kbeta.astype(BF16), k2b, preferred_element_type=F32)
    t = _unit_lower_inverse(jnp.where(r_i > c_i, kk * dec, 0.0), GLA_SUB)
    ecum = jnp.exp(cum)
    sol = _bmm3(t, jnp.concatenate([v2 * beta, kbeta * ecum], axis=-1))
    k_wide = jnp.concatenate([k2b, jnp.zeros_like(k2b)], axis=1)
    qk = jnp.einsum('gtd,gsd->gts', q2.astype(BF16), k_wide, preferred_element_type=F32)
    att = jnp.where(r_w >= c_w, qk * dec_w, 0.0)
    last = cum[:, C - 1:C]

    u_ref[...] = _unheads(sol[:, :, :d])
    w_ref[...] = _unheads(sol[:, :, d:]).astype(BF16)
    qe_ref[...] = _unheads(q2 * ecum).astype(BF16)
    kd_ref[...] = _unheads(k2 * jnp.exp(last - cum)).astype(BF16)
    att_ref[...] = _unheads(att).astype(BF16)

    @pl.when(c == pl.num_programs(2) - 1)
    def _():
        cq_ref[0] = prev_ref[0]
        ck_ref[0] = prev_ref[1]
        cv_ref[0] = prev_ref[2]


def _dn_scan_kernel(u_ref, w_ref, qe_ref, kd_ref, att_ref, cum_ref, z_ref, nw_ref, *rest, a_lane, group):
    o_ref, st_ref, s_ref = rest[-3:]
    c = pl.program_id(1)
    C = u_ref.shape[0]
    H = DN_HEADS

    @pl.when(c == 0)
    def _():
        s_ref[...] = jnp.zeros_like(s_ref)

    s = s_ref[...]
    sb = s.astype(BF16)
    v_new = _heads(u_ref[...], H) - jnp.einsum('hcd,hdv->hcv', _heads(w_ref[...], H), sb,
                                               preferred_element_type=F32)
    vb = v_new.astype(BF16)
    o = jnp.einsum('hts,hsv->htv', _heads(att_ref[...], H)[:, :, :C], vb, preferred_element_type=F32)
    o = o + jnp.einsum('hcd,hdv->hcv', _heads(qe_ref[...], H), sb, preferred_element_type=F32)
    cum_last = cum_ref[C - 1:C, :]
    lanes = [(h // group) * LANE + a_lane + h for h in range(H)]
    e_last = jnp.exp(jnp.stack([cum_last[:, l:l + 1] for l in lanes], axis=0))
    s_new = s * e_last + jnp.einsum('hcd,hcv->hdv', _heads(kd_ref[...], H), vb, preferred_element_type=F32)
    s_ref[...] = s_new
    on = o * lax.rsqrt(jnp.mean(o * o, axis=-1, keepdims=True) + RMS_EPS) * nw_ref[...]
    o_ref[...] = _unheads(on * _silu(_heads(z_ref[...], H))).astype(o_ref.dtype)

    @pl.when(c == pl.num_programs(1) - 1)
    def _():
        st_ref[0] = s_new


def _dn_prompt(proj, small, cols, conv_w, j, alog_pad, dtb_pad, dn_norm, mix, *, batch, seq):
    nc = seq // CHUNK
    assert seq % CHUNK == 0
    G, d = DN_GROUP, DN_HEAD
    gw = G * d
    ng = DN_HEADS // G
    dn_k = DN_HEADS * d
    t = batch * seq
    qb, kb, vb = cols["dn_q"] // gw, cols["dn_k"] // gw, cols["dn_v"] // gw
    row = lambda b, g, c: b * nc + c
    wide = lambda dt: jax.ShapeDtypeStruct((t, dn_k), dt)
    tok = pl.BlockSpec((CHUNK, gw), lambda b, g, c: (row(b, g, c), g))
    tail = pl.BlockSpec((1, 8, gw), lambda b, g, c: (b, 0, g))
    u, w, qe, kd, att, cum, cq, ck, cv = pl.pallas_call(
        functools.partial(_dn_prep_kernel, a_lane=cols["a_lane"], b_lane=cols["b_lane"]),
        out_shape=(wide(F32), wide(BF16), wide(BF16), wide(BF16), wide(BF16),
                   jax.ShapeDtypeStruct((t, ng * LANE), F32),
                   jax.ShapeDtypeStruct((batch, 8, dn_k), F32), jax.ShapeDtypeStruct((batch, 8, dn_k), F32),
                   jax.ShapeDtypeStruct((batch, 8, dn_k), F32)),
        grid=(batch, ng, nc),
        in_specs=[
            pl.BlockSpec((CHUNK, gw), lambda b, g, c: (row(b, g, c), qb + g)),
            pl.BlockSpec((CHUNK, gw), lambda b, g, c: (row(b, g, c), kb + g)),
            pl.BlockSpec((CHUNK, gw), lambda b, g, c: (row(b, g, c), vb + g)),
            pl.BlockSpec((CHUNK, LANE), lambda b, g, c: (row(b, g, c), 0)),
            pl.BlockSpec((None, CONV_WIDTH, gw), lambda b, g, c: (j, 0, g)),
            pl.BlockSpec((None, CONV_WIDTH, gw), lambda b, g, c: (j, 0, dn_k // gw + g)),
            pl.BlockSpec((None, CONV_WIDTH, gw), lambda b, g, c: (j, 0, 2 * dn_k // gw + g)),
            pl.BlockSpec((1, LANE), lambda b, g, c: (0, 0)),
            pl.BlockSpec((1, LANE), lambda b, g, c: (0, 0)),
        ],
        out_specs=(tok, tok, tok, tok, tok,
                   pl.BlockSpec((CHUNK, LANE), lambda b, g, c: (row(b, g, c), g)), tail, tail, tail),
        scratch_shapes=[pltpu.VMEM((3, 8, gw), F32)],
        compiler_params=_cparams(("parallel", "parallel", "arbitrary")),
        name="deltanet_prep",
    )(proj, proj, proj, small, conv_w, conv_w, conv_w, alog_pad, dtb_pad)

    full = lambda width: pl.BlockSpec((CHUNK, width), lambda b, c: (b * nc + c, 0))
    in_specs = [full(dn_k), full(dn_k), full(dn_k), full(dn_k), full(dn_k), full(ng * LANE),
                pl.BlockSpec((CHUNK, dn_k), lambda b, c: (b * nc + c, cols["dn_z"] // dn_k)),
                pl.BlockSpec((1, d), lambda b, c: (0, 0))]
    args = [u, w, qe, kd, att, cum, proj, dn_norm.reshape(1, -1)]
    aliases = _alias_outputs(in_specs, args, [mix])
    mix, st = pl.pallas_call(
        functools.partial(_dn_scan_kernel, a_lane=cols["a_lane"], group=G),
        out_shape=(jax.ShapeDtypeStruct(mix.shape, BF16),
                   jax.ShapeDtypeStruct((batch, DN_HEADS, d, d), F32)),
        grid=(batch, nc),
        in_specs=in_specs,
        out_specs=(pl.BlockSpec((CHUNK, dn_k), lambda b, c: (b * nc + c, mix.shape[1] // dn_k - 1)),
                   pl.BlockSpec((1, DN_HEADS, d, d), lambda b, c: (b, 0, 0, 0))),
        scratch_shapes=[pltpu.VMEM((DN_HEADS, d, d), F32)],
        input_output_aliases=aliases,
        compiler_params=_cparams(("parallel", "arbitrary")),
        name="deltanet_scan",
    )(*args)
    tail_rows = slice(8 - (CONV_WIDTH - 1), 8)
    conv_new = jnp.concatenate([cq[:, tail_rows], ck[:, tail_rows], cv[:, tail_rows]], axis=-1)
    return mix, st, conv_new


def _dn_sample_kernel(q_ref, k_ref, v_ref, z_ref, sm_ref, bq_ref, bk_ref, bv_ref, wq_ref, wk_ref, wv_ref,
                      alog_ref, dtb_ref, nw_ref, s_ref, *rest, a_lane, b_lane):
    o_ref, so_ref = rest[-2:]
    hg = pl.program_id(1)
    nb = q_ref.shape[0]
    G = s_ref.shape[1]
    d = DN_HEAD

    def conv(x_ref, buf_ref, w_ref):
        y = x_ref[...] * w_ref[CONV_WIDTH - 1:CONV_WIDTH]
        for i in range(CONV_WIDTH - 1):
            y = y + buf_ref[:, i, :] * w_ref[i:i + 1]
        return _silu(y)

    yq = conv(q_ref, bq_ref, wq_ref)
    yk = conv(k_ref, bk_ref, wk_ref)
    yv = conv(v_ref, bv_ref, wv_ref)
    decay, beta_all = _dn_gates(sm_ref[...], alog_ref, dtb_ref)
    z = z_ref[...]
    outs = []
    for hl in range(G):
        head = hg * G + hl
        sl = slice(hl * d, (hl + 1) * d)
        q2 = _l2norm(yq[:, sl]) * (d ** -0.5)
        k2 = _l2norm(yk[:, sl])
        v2 = yv[:, sl]
        eg = jnp.exp(_lane_col(decay, a_lane + head))
        beta = _lane_col(beta_all, b_lane + head)
        qT = q2.T
        kT = k2.T
        rows = []
        for b in range(nb):
            s = s_ref[b, hl]
            kc = kT[:, b:b + 1]
            ks = jnp.sum(kc * s, axis=0, keepdims=True)
            v_new = beta[b:b + 1] * (v2[b:b + 1] - eg[b:b + 1] * ks)
            s_new = s * eg[b:b + 1] + kc * v_new
            so_ref[b, hl] = s_new
            rows.append(jnp.sum(qT[:, b:b + 1] * s_new, axis=0, keepdims=True))
        outs.append(_gated_head_norm(jnp.concatenate(rows, axis=0), nw_ref, z[:, sl]))
    o_ref[...] = jnp.concatenate(outs, axis=1).astype(o_ref.dtype)


def _dn_sample(proj, small, cols, conv_w, j, alog_pad, dtb_pad, dn_norm, states, conv_bufs, prev, mix, *, row0):
    nb = SAMPLE_ROWS
    bs = states.shape[1]
    G, d = DN_GROUP, DN_HEAD
    gw = G * d
    ng = DN_HEADS // G
    dn_k = DN_HEADS * d
    assert bs % nb == 0 and row0 % nb == 0
    r0 = row0 // nb
    qb, kb, vb, zb = cols["dn_q"] // gw, cols["dn_k"] // gw, cols["dn_v"] // gw, cols["dn_z"] // gw
    ob = (mix.shape[1] - dn_k) // gw
    nbuf = CONV_WIDTH - 1
    in_specs = [
        pl.BlockSpec((nb, gw), lambda i, g: (r0 + i, qb + g)),
        pl.BlockSpec((nb, gw), lambda i, g: (r0 + i, kb + g)),
        pl.BlockSpec((nb, gw), lambda i, g: (r0 + i, vb + g)),
        pl.BlockSpec((nb, gw), lambda i, g: (r0 + i, zb + g)),
        pl.BlockSpec((nb, LANE), lambda i, g: (r0 + i, 0)),
        pl.BlockSpec((None, nb, nbuf, gw), lambda i, g: (j, i, 0, g)),
        pl.BlockSpec((None, nb, nbuf, gw), lambda i, g: (j, i, 0, dn_k // gw + g)),
        pl.BlockSpec((None, nb, nbuf, gw), lambda i, g: (j, i, 0, 2 * dn_k // gw + g)),
        pl.BlockSpec((None, CONV_WIDTH, gw), lambda i, g: (j, 0, g)),
        pl.BlockSpec((None, CONV_WIDTH, gw), lambda i, g: (j, 0, dn_k // gw + g)),
        pl.BlockSpec((None, CONV_WIDTH, gw), lambda i, g: (j, 0, 2 * dn_k // gw + g)),
        pl.BlockSpec((1, LANE), lambda i, g: (0, 0)),
        pl.BlockSpec((1, LANE), lambda i, g: (0, 0)),
        pl.BlockSpec((1, d), lambda i, g: (0, 0)),
        pl.BlockSpec((None, nb, G, d, d), lambda i, g: (j, i, g, 0, 0)),
    ]
    args = [proj, proj, proj, proj, small, conv_bufs, conv_bufs, conv_bufs, conv_w, conv_w, conv_w,
            alog_pad, dtb_pad, dn_norm.reshape(1, -1), states]
    aliases = _alias_outputs(in_specs, args, [mix, prev])
    return pl.pallas_call(
        functools.partial(_dn_sample_kernel, a_lane=cols["a_lane"], b_lane=cols["b_lane"]),
        out_shape=(jax.ShapeDtypeStruct(mix.shape, BF16),
                   jax.ShapeDtypeStruct(states.shape, F32)),
        grid=(bs // nb, ng),
        in_specs=in_specs,
        out_specs=(pl.BlockSpec((nb, gw), lambda i, g: (r0 + i, ob + g)),
                   pl.BlockSpec((None, nb, G, d, d), lambda i, g: (j, i, g, 0, 0))),
        input_output_aliases=aliases,
        compiler_params=_cparams(("parallel", "parallel")),
        name="deltanet_sample",
    )(*args)


def _mla_post_kernel(ckv_ref, kpe_ref, cs_ref, g_ref, lat_ref, kr_ref):
    x = ckv_ref[...]
    lat_ref[...] = x * lax.rsqrt(jnp.mean(x * x, axis=-1, keepdims=True) + RMS_EPS) * g_ref[...]
    zz = kpe_ref[...] * cs_ref[...]
    kr_ref[...] = zz + pltpu.roll(zz, QK_ROPE, 1)


def _mla_post(a, cs, kv_norm, *, rows=520):
    m = a.shape[0]
    assert m % rows == 0
    ckv_blk = Q_LORA // KV_LORA
    kpe_blk = (Q_LORA + KV_LORA) // LANE
    return pl.pallas_call(
        _mla_post_kernel,
        out_shape=(jax.ShapeDtypeStruct((m, KV_LORA), F32), jax.ShapeDtypeStruct((m, LANE), F32)),
        grid=(m // rows,),
        in_specs=[pl.BlockSpec((rows, KV_LORA), lambda i: (i, ckv_blk)),
                  pl.BlockSpec((rows, LANE), lambda i: (i, kpe_blk)),
                  pl.BlockSpec((rows, LANE), lambda i: (i, 0)),
                  pl.BlockSpec((1, KV_LORA), lambda i: (0, 0))],
        out_specs=(pl.BlockSpec((rows, KV_LORA), lambda i: (i, 0)),
                   pl.BlockSpec((rows, LANE), lambda i: (i, 0))),
        compiler_params=_cparams(("parallel",)),
        name="mla_latent_rope",
    )(a, a, cs, kv_norm.reshape(1, -1))


def _mla_prefill_kernel(qn_ref, qr_ref, cs_ref, kn_ref, v_ref, kr_ref, o_ref, *, scale, tile):
    n = qn_ref.shape[0] // tile
    kf = jnp.concatenate([kn_ref[...], kr_ref[...].astype(BF16)], axis=1)
    qloc = lax.broadcasted_iota(jnp.int32, (tile, tile), 0)
    kloc = lax.broadcasted_iota(jnp.int32, (tile, tile), 1)
    for qi in range(n):
        rows = slice(qi * tile, (qi + 1) * tile)
        q = jnp.concatenate([(qn_ref[rows, :] * scale).astype(BF16),
                             (qr_ref[rows, :] * cs_ref[rows, :] * scale).astype(BF16)], axis=1)
        for kj in range(qi + 1):
            ks = slice(kj * tile, (kj + 1) * tile)
            s = _nt(q, kf[ks])
            if kj == qi:
                s = jnp.where(kloc <= qloc, s, -jnp.inf)
            s_max = jnp.max(s, axis=-1, keepdims=True)
            if kj == 0:
                m = s_max
                p = jnp.exp(s - m)
                l = jnp.sum(p, axis=-1, keepdims=True)
                acc = jnp.dot(p.astype(BF16), v_ref[ks, :], preferred_element_type=F32)
            else:
                m_new = jnp.maximum(m, s_max)
                corr = jnp.exp(m - m_new)
                p = jnp.exp(s - m_new)
                l = l * corr + jnp.sum(p, axis=-1, keepdims=True)
                acc = acc * corr + jnp.dot(p.astype(BF16), v_ref[ks, :], preferred_element_type=F32)
                m = m_new
        o_ref[rows, :] = (acc / l).astype(o_ref.dtype)


def _mla_prefill(q, cs, kv, kr2, *, batch, seq, out_rows):
    assert seq % ATT_TQ == 0
    h = MLA_HEADS
    scale = (QK_NOPE + QK_ROPE) ** -0.5
    return pl.pallas_call(
        functools.partial(_mla_prefill_kernel, scale=scale, tile=ATT_TQ),
        out_shape=jax.ShapeDtypeStruct((out_rows, h * V_HEAD), BF16),
        grid=(batch, h),
        in_specs=[
            pl.BlockSpec((seq, QK_NOPE), lambda b, hh: (b, hh)),
            pl.BlockSpec((seq, LANE), lambda b, hh: (b, h + hh)),
            pl.BlockSpec((seq, LANE), lambda b, hh: (b, 0)),
            pl.BlockSpec((seq, QK_NOPE), lambda b, hh: (b, 2 * hh)),
            pl.BlockSpec((seq, V_HEAD), lambda b, hh: (b, 2 * hh + 1)),
            pl.BlockSpec((seq, LANE), lambda b, hh: (b, 0)),
        ],
        out_specs=pl.BlockSpec((seq, V_HEAD), lambda b, hh: (b, hh)),
        compiler_params=_cparams(("parallel", "parallel")),
        name="mla_prefill",
    )(q, q, cs, kv, kv, kr2)


def _absorb_q_kernel(x_ref, w_ref, o_ref):
    o_ref[...] = _nt(x_ref[...].astype(BF16), w_ref[...].astype(BF16)).astype(o_ref.dtype)


def _absorb_q(q, w_kvb, j, *, row0, rows):
    h = MLA_HEADS
    assert row0 % rows == 0
    return pl.pallas_call(
        _absorb_q_kernel,
        out_shape=jax.ShapeDtypeStruct((rows, h * KV_LORA), BF16),
        grid=(h,),
        in_specs=[pl.BlockSpec((rows, QK_NOPE), lambda hh: (row0 // rows, hh)),
                  pl.BlockSpec((None, KV_LORA, QK_NOPE), lambda hh: (j, 0, 2 * hh))],
        out_specs=pl.BlockSpec((rows, KV_LORA), lambda hh: (0, hh)),
        compiler_params=_cparams(("parallel",)),
        name="mla_absorb_q",
    )(q, w_kvb)


def _absorb_v_kernel(x_ref, w_ref, *rest):
    o_ref = rest[-1]
    o_ref[...] = jnp.dot(x_ref[...].astype(BF16), w_ref[...].astype(BF16),
                         preferred_element_type=F32).astype(o_ref.dtype)


def _absorb_v(o_lat, w_kvb, j, att, *, row0):
    rows = o_lat.shape[0]
    h = MLA_HEADS
    assert row0 % rows == 0
    in_specs = [pl.BlockSpec((rows, KV_LORA), lambda hh: (0, hh)),
                pl.BlockSpec((None, KV_LORA, V_HEAD), lambda hh: (j, 0, 2 * hh + 1))]
    args = [o_lat, w_kvb]
    aliases = _alias_outputs(in_specs, args, [att])
    return pl.pallas_call(
        _absorb_v_kernel,
        out_shape=jax.ShapeDtypeStruct(att.shape, BF16),
        grid=(h,),
        in_specs=in_specs,
        out_specs=pl.BlockSpec((rows, V_HEAD), lambda hh: (row0 // rows, hh)),
        input_output_aliases=aliases,
        compiler_params=_cparams(("parallel",)),
        name="mla_absorb_v",
    )(*args)


def _mla_decode_kernel(pt_ref, ql_ref, qr_ref, cs_ref, cn_ref, rn_ref, *rest, scale, n_pages):
    lat_refs = rest[:n_pages]
    rope_refs = rest[n_pages:2 * n_pages]
    o_ref, m_ref, l_ref, acc_ref = rest[2 * n_pages:]
    g = pl.program_id(1)
    ql = ql_ref[0]
    zz = qr_ref[0] * cs_ref[...]

    @pl.when(g == 0)
    def _():
        s0 = (jnp.sum(ql.astype(F32) * cn_ref[0], axis=-1, keepdims=True)
              + jnp.sum(zz * rn_ref[0], axis=-1, keepdims=True)) * scale
        m_ref[...] = s0
        l_ref[...] = jnp.ones_like(l_ref)
        acc_ref[...] = jnp.broadcast_to(cn_ref[0], acc_ref.shape)

    qrot = (zz + pltpu.roll(zz, QK_ROPE, 1))[:, :QK_ROPE].astype(BF16)
    cbs = [lat_refs[i][0, 0].astype(BF16) for i in range(n_pages)]
    s = jnp.concatenate(
        [_nt(ql, cbs[i]) + jnp.dot(qrot, rope_refs[i][0, 0].astype(BF16), preferred_element_type=F32)
         for i in range(n_pages)], axis=1) * scale
    m = m_ref[...]
    m_new = jnp.maximum(m, jnp.max(s, axis=-1, keepdims=True))
    corr = jnp.exp(m - m_new)
    e = jnp.exp(s - m_new)
    l_ref[...] = l_ref[...] * corr + jnp.sum(e, axis=-1, keepdims=True)
    eb = e.astype(BF16)
    page = cbs[0].shape[0]
    pv = jnp.dot(eb[:, :page], cbs[0], preferred_element_type=F32)
    for i in range(1, n_pages):
        pv = pv + jnp.dot(eb[:, i * page:(i + 1) * page], cbs[i], preferred_element_type=F32)
    acc_ref[...] = acc_ref[...] * corr + pv
    m_ref[...] = m_new

    @pl.when(g == pl.num_programs(1) - 1)
    def _():
        o_ref[0] = acc_ref[...] / l_ref[...]


def _mla_decode(q_lat, q_rope, cs_row, lat_new, kr_new, cache_latent, cache_k_rope_t, page_table, layer):
    bs = q_lat.shape[0]
    h = MLA_HEADS
    n_log = page_table.shape[1]
    page = cache_latent.shape[2]
    npg = DEC_PAGES
    assert n_log % npg == 0
    scale = (QK_NOPE + QK_ROPE) ** -0.5

    def page_spec(rows, width, i):
        return pl.BlockSpec((1, 1, rows, width),
                            lambda b, g, pt: (layer, pt[b * n_log + g * npg + i], 0, 0))

    in_specs = [
        pl.BlockSpec((1, h, KV_LORA), lambda b, g, pt: (b, 0, 0)),
        pl.BlockSpec((1, h, LANE), lambda b, g, pt: (b, 0, 0)),
        pl.BlockSpec((1, LANE), lambda b, g, pt: (0, 0)),
        pl.BlockSpec((1, 1, KV_LORA), lambda b, g, pt: (b, 0, 0)),
        pl.BlockSpec((1, 1, LANE), lambda b, g, pt: (b, 0, 0)),
    ]
    in_specs += [page_spec(page, KV_LORA, i) for i in range(npg)]
    in_specs += [page_spec(QK_ROPE, page, i) for i in range(npg)]
    grid_spec = pltpu.PrefetchScalarGridSpec(
        num_scalar_prefetch=1,
        grid=(bs, n_log // npg),
        in_specs=in_specs,
        out_specs=pl.BlockSpec((1, h, KV_LORA), lambda b, g, pt: (b, 0, 0)),
        scratch_shapes=[pltpu.VMEM((h, 1), F32), pltpu.VMEM((h, 1), F32), pltpu.VMEM((h, KV_LORA), F32)],
    )
    return pl.pallas_call(
        functools.partial(_mla_decode_kernel, scale=scale, n_pages=npg),
        out_shape=jax.ShapeDtypeStruct((bs, h, KV_LORA), F32),
        grid_spec=grid_spec,
        compiler_params=_cparams(("parallel", "arbitrary")),
        name="mla_decode",
    )(page_table.reshape(-1), q_lat.reshape(bs, h, KV_LORA), q_rope.reshape(bs, h, LANE), cs_row,
      lat_new.reshape(bs, 1, KV_LORA), kr_new.reshape(bs, 1, LANE),
      *([cache_latent] * npg), *([cache_k_rope_t] * npg))


def _rot_cols(w):
    half = QK_ROPE // 2
    return jnp.concatenate([-w[..., half:], w[..., :half]], axis=-1)


def _ab_columns(d_model):
    gla_k = GLA_HEADS * (d_model // 16)
    gla_v = GLA_HEADS * (d_model // 8)
    dn = DN_HEADS * DN_HEAD
    aligned = 2 * gla_k + gla_v
    n_main = aligned + gla_v + 4 * dn
    cols = dict(gq=0, gk=gla_k, gv=2 * gla_k, g_out=aligned, dn_q=aligned + gla_v, dn_k=aligned + gla_v + dn,
                dn_v=aligned + gla_v + 2 * dn, dn_z=aligned + gla_v + 3 * dn,
                a_lane=GLA_GATE_RANK, b_lane=GLA_GATE_RANK + DN_HEADS)
    return cols, aligned, n_main


def _ab_small_weight(w_all, j, aligned, n_main):
    w = w_all[j]
    lo = w[:, aligned:aligned + GLA_GATE_RANK]
    hi = w[:, n_main + GLA_GATE_RANK:]
    pad = jnp.zeros((w.shape[0], LANE - GLA_GATE_RANK - hi.shape[1]), w.dtype)
    return jnp.concatenate([lo, hi, pad], axis=1)


def _pad_lanes(v, lane0):
    return jnp.zeros((1, LANE), F32).at[0, lane0:lane0 + v.shape[0]].set(v.astype(F32))


def _rope_table(pos):
    half = QK_ROPE // 2
    inv = ROPE_THETA ** (-jnp.arange(half, dtype=F32) / half)
    ang = pos.astype(F32)[:, None] * inv[None, :]
    cos, sin = jnp.cos(ang), jnp.sin(ang)
    return jnp.concatenate([cos, cos, sin, sin], axis=1)


def _mla_mixer(x, cs, norm_w, w_a, q_norm, kv_norm, w_qb, w_kvb, w_o, cache_latent, cache_k_rope_t,
               page_table, j, *, bp, lp, bs):
    tp = bp * lp
    d_model = x.shape[1]
    kpe_w = w_a[:, Q_LORA + KV_LORA:]
    a_pad = jnp.zeros((d_model, -(Q_LORA + KV_LORA + LANE) % 512), F32)
    w_a2 = jnp.concatenate([w_a[:, :Q_LORA + KV_LORA], kpe_w, _rot_cols(kpe_w), a_pad], axis=1)
    a = _matmul(x, [w_a2], tn=512, out_dtype=F32, gain=norm_w, name="mla_a_proj")
    latent, kr2 = _mla_post(a, cs, kv_norm)
    w_qb = w_qb.reshape(Q_LORA, MLA_HEADS, QK_NOPE + QK_ROPE)
    w_rope = w_qb[..., QK_NOPE:]
    w_q2 = jnp.concatenate(
        [w_qb[..., :QK_NOPE].reshape(Q_LORA, -1),
         jnp.concatenate([w_rope, _rot_cols(w_rope)], axis=-1).reshape(Q_LORA, -1)], axis=1)
    q = _matmul(a, [w_q2], tn=512, out_dtype=F32, gain=q_norm, name="mla_q_proj")
    kv = _matmul(latent, [w_kvb], tn=512, out_dtype=BF16, layer=j, name="mla_kv_proj")
    att = _mla_prefill(q, cs, kv, kr2, batch=bp, seq=lp, out_rows=x.shape[0])
    q_lat = _absorb_q(q, w_kvb, j, row0=tp, rows=bs)
    q_rope_s = q[tp:, MLA_HEADS * QK_NOPE:]
    o_lat = _mla_decode(q_lat, q_rope_s, cs[tp:tp + 1], latent[tp:], kr2[tp:],
                        cache_latent, cache_k_rope_t, page_table, j)
    att = _absorb_v(o_lat.reshape(bs, MLA_HEADS * KV_LORA), w_kvb, j, att, row0=tp)
    x = _matmul(att, [w_o], tn=512, out_dtype=F32, resid=x, layer=j, name="mla_out_proj")
    return x, latent, kr2


def kernel(x_prompt, x_sample, state_gla, state_delta, state_conv, cache_latent, cache_k_rope, page_table,
           norm_mix, norm_ffn, norm_final, ab_w_in, gla_w_gate, gla_b_gate, gla_norm, dn_conv_w,
           dn_a_log, dn_dt_bias, dn_norm, ab_w_out, mla_w_a, mla_q_norm, mla_kv_norm, mla_w_qb,
           mla_w_kvb, mla_w_o, ffn_w_gate, ffn_w_up, ffn_w_down):
    bp, lp, d_model = x_prompt.shape
    bs, ls, _ = x_sample.shape
    assert ls == 1
    tp = bp * lp
    depth = norm_mix.shape[0]
    past_len = page_table.shape[1] * cache_latent.shape[2]
    gla_dk, gla_dv = state_gla.shape[3:]
    dn_ch = state_conv.shape[-1]

    x = jnp.concatenate([x_prompt.reshape(tp, d_model), x_sample.reshape(bs, d_model)], axis=0)
    pos = jnp.concatenate([jnp.tile(jnp.arange(lp), bp), jnp.full((bs,), past_len)])
    cs = _rope_table(pos)

    cache_k_rope_t = jnp.swapaxes(cache_k_rope, 2, 3)
    gla_p, dn_p, cv_p, cv_s = [], [], [], []
    lat_p, lat_s, kr_p, kr_s = [], [], [], []
    gla_s = dn_s = None
    for layer in range(depth):
        j = layer // 2
        if layer % 2 == 0:
            cols, aligned, n_main = _ab_columns(d_model)
            proj = _ab_in_proj(x, norm_mix[layer], ab_w_in, j, aligned=aligned, shift=GLA_GATE_RANK, n_out=n_main)
            small = _matmul(x, [_ab_small_weight(ab_w_in, j, aligned, n_main)], tn=LANE, out_dtype=F32,
                            gain=norm_mix[layer], name="ab_in_small")
            wg_pad = jnp.zeros((LANE, gla_w_gate.shape[2]), F32).at[:GLA_GATE_RANK].set(gla_w_gate[j])
            alog_pad = _pad_lanes(dn_a_log[j], cols["a_lane"])
            dtb_pad = _pad_lanes(dn_dt_bias[j], cols["a_lane"])
            mix, g1 = _gla_prompt(proj, small, cols, wg_pad, gla_b_gate[j], gla_norm[j], batch=bp, seq=lp,
                                  dk=gla_dk, dv=gla_dv, mix_shape=(tp + bs, GLA_HEADS * gla_dv + DN_HEADS * DN_HEAD))
            mix, gla_s = _gla_sample(proj, small, cols, wg_pad, gla_b_gate[j], gla_norm[j], state_gla, j, gla_s, mix,
                                     row0=tp, dk=gla_dk, dv=gla_dv)
            mix, d1, c1 = _dn_prompt(proj, small, cols, dn_conv_w, j, alog_pad, dtb_pad, dn_norm[j], mix,
                                     batch=bp, seq=lp)
            mix, dn_s = _dn_sample(proj, small, cols, dn_conv_w, j, alog_pad, dtb_pad, dn_norm[j],
                                   state_delta, state_conv, dn_s, mix, row0=tp)
            c2 = jnp.concatenate([state_conv[j, :, 1:], proj[tp:, None, cols["dn_q"]:cols["dn_q"] + dn_ch]], axis=1)
            x = _matmul(mix, [ab_w_out], tn=512, out_dtype=F32, resid=x, layer=j, name="ab_out_proj")
            gla_p.append(g1)
            dn_p.append(d1)
            cv_p.append(c1); cv_s.append(c2)
        else:
            x, latent, kr2 = _mla_mixer(x, cs, norm_mix[layer], mla_w_a[j], mla_q_norm[j], mla_kv_norm[j],
                                        mla_w_qb[j], mla_w_kvb, mla_w_o, cache_latent, cache_k_rope_t,
                                        page_table, j, bp=bp, lp=lp, bs=bs)
            lat_p.append(latent[:tp].reshape(bp, lp, KV_LORA)); lat_s.append(latent[tp:].reshape(bs, 1, KV_LORA))
            kr_p.append(kr2[:tp, :QK_ROPE].reshape(bp, lp, QK_ROPE)); kr_s.append(kr2[tp:, :QK_ROPE].reshape(bs, 1, QK_ROPE))
        hid = _matmul(x, [ffn_w_gate, ffn_w_up], tn=256, out_dtype=BF16, gain=norm_ffn[layer], layer=layer,
                      name="ffn_gate_up")
        x = _matmul(hid, [ffn_w_down], tn=256, out_dtype=F32, resid=x, layer=layer, name="ffn_down")
    y_prompt = _rmsnorm_rows(x, norm_final, row0=0, n_rows=tp, rows=256).reshape(bp, lp, d_model)
    y_sample = _rmsnorm_rows(x, norm_final, row0=tp, n_rows=bs, rows=bs).reshape(bs, 1, d_model)
    return (y_prompt, y_sample, jnp.stack(gla_p), gla_s, jnp.stack(dn_p), dn_s,
            jnp.stack(cv_p), jnp.stack(cv_s), jnp.stack(lat_p), jnp.stack(lat_s), jnp.stack(kr_p), jnp.stack(kr_s))
```

```python
import functools
import math

import numpy as np
import jax
import jax.numpy as jnp
from jax import lax
from jax.experimental import pallas as pl
from jax.experimental.pallas import tpu as pltpu

F32 = jnp.float32
BF16 = jnp.bfloat16
HIGHEST = lax.Precision.HIGHEST

GLA_HEADS = 4
GLA_GATE_RANK = 16
GLA_GATE_TAU = 16.0
DN_HEADS = 16
DN_HEAD = 128
CONV_WIDTH = 4
CHUNK = 64
MLA_HEADS = 32
Q_LORA = 1024
KV_LORA = 512
QK_NOPE = 128
QK_ROPE = 64
V_HEAD = 128
ROPE_THETA = 10000.0
RMS_EPS = 1e-6
L2_EPS = 1e-6

LANE = 128
VMEM_LIMIT_BYTES = 60 * 1024 * 1024

ROW_TILE = 1040
NORM_ROWS = 16
GLA_SUB = 16
GLA_STEP_CHUNKS = 4
DN_GROUP = 8
ATT_TQ = 512
DEC_PAGES = 16
SAMPLE_ROWS = 8


def _cparams(sem):
    return pltpu.CompilerParams(dimension_semantics=sem, vmem_limit_bytes=VMEM_LIMIT_BYTES)


def _silu(x):
    return x * jax.nn.sigmoid(x)


def _log_sigmoid(x):
    return jnp.minimum(x, 0.0) - jnp.log(1.0 + jnp.exp(-jnp.abs(x)))


def _softplus(x):
    return jnp.maximum(x, 0.0) + jnp.log(1.0 + jnp.exp(-jnp.abs(x)))


def _nt(a, b):
    return lax.dot_general(a, b, (((1,), (1,)), ((), ())), preferred_element_type=F32)


def _tn(a, b):
    return lax.dot_general(a, b, (((0,), (0,)), ((), ())), preferred_element_type=F32)


def _dotx(a, b):
    return jnp.dot(a, b, precision=HIGHEST, preferred_element_type=F32)


def _stage_lhs(x_ref, g_ref, xs_ref):
    @pl.when(pl.program_id(1) == 0)
    def _():
        def body(r, carry):
            sl = pl.ds(pl.multiple_of(r * NORM_ROWS, NORM_ROWS), NORM_ROWS)
            x = x_ref[sl, :].astype(F32)
            if g_ref is not None:
                x = x * lax.rsqrt(jnp.mean(x * x, axis=-1, keepdims=True) + RMS_EPS) * g_ref[...]
            xs_ref[sl, :] = x.astype(BF16)
            return carry
        trips = x_ref.shape[0] // NORM_ROWS
        lax.fori_loop(0, trips, body, 0, unroll=max(u for u in (5, 4, 2, 1) if trips % u == 0))


def _mm_kernel(*refs, has_norm, has_resid, n_w, stage):
    it = iter(refs)
    x_ref = next(it)
    g_ref = next(it) if has_norm else None
    w_refs = [next(it) for _ in range(n_w)]
    r_ref = next(it) if has_resid else None
    o_ref = next(it)
    xs_ref = next(it) if stage else None

    if stage:
        _stage_lhs(x_ref, g_ref, xs_ref)
        xb = xs_ref[...]
    else:
        xb = x_ref[...]
    acc = [jnp.dot(xb, w[...].astype(BF16), preferred_element_type=F32) for w in w_refs]
    y = _silu(acc[0]) * acc[1] if n_w == 2 else acc[0]
    if has_resid:
        y = y + r_ref[...]
    o_ref[...] = y.astype(o_ref.dtype)


def _matmul(x, ws, *, tn, out_dtype, gain=None, resid=None, layer=None, name):
    m = x.shape[0]
    k, n = ws[0].shape[-2:]
    tm = ROW_TILE
    assert m % tm == 0 and n % tn == 0 and k % LANE == 0 and tm % NORM_ROWS == 0
    stage = gain is not None or x.dtype != BF16
    in_specs = [pl.BlockSpec((tm, k), lambda i, j: (i, 0), pipeline_mode=pl.Buffered(1))]
    args = [x]
    if gain is not None:
        in_specs.append(pl.BlockSpec((1, k), lambda i, j: (0, 0)))
        args.append(gain.reshape(1, k).astype(F32))
    for w in ws:
        if layer is None:
            in_specs.append(pl.BlockSpec((k, tn), lambda i, j: (0, j)))
        else:
            in_specs.append(pl.BlockSpec((None, k, tn), lambda i, j: (layer, 0, j)))
        args.append(w)
    if resid is not None:
        in_specs.append(pl.BlockSpec((tm, tn), lambda i, j: (i, j)))
        args.append(resid)
    scratch = [pltpu.VMEM((tm, k), BF16)] if stage else []
    return pl.pallas_call(
        functools.partial(_mm_kernel, has_norm=gain is not None, has_resid=resid is not None,
                          n_w=len(ws), stage=stage),
        out_shape=jax.ShapeDtypeStruct((m, n), out_dtype),
        grid=(m // tm, n // tn),
        in_specs=in_specs,
        out_specs=pl.BlockSpec((tm, tn), lambda i, j: (i, j)),
        scratch_shapes=scratch,
        compiler_params=_cparams(("parallel", "arbitrary")),
        name=name,
    )(*args)


def _ab_in_kernel(x_ref, g_ref, w_ref, wx_ref, ws_ref, o_ref, xs_ref, *, n_plain, n_main, shift):
    j = pl.program_id(1)
    tm, tn = o_ref.shape
    _stage_lhs(x_ref, g_ref, xs_ref)

    @pl.when(j < n_plain)
    def _():
        o_ref[...] = jnp.dot(xs_ref[...], w_ref[...].astype(BF16), preferred_element_type=F32)

    @pl.when((j >= n_plain) & (j < n_main))
    def _():
        w = jnp.concatenate([w_ref[...], wx_ref[...]], axis=1).astype(BF16)[:, shift:shift + tn]
        o_ref[...] = jnp.dot(xs_ref[...], w, preferred_element_type=F32)

    @pl.when(j == n_main)
    def _():
        y = jnp.dot(xs_ref[...], ws_ref[...].astype(BF16), preferred_element_type=F32)
        o_ref[...] = jnp.concatenate([y, jnp.zeros((tm, tn - y.shape[1]), F32)], axis=1)


def _ab_in_proj(x, gain, w_all, layer, w_small, *, aligned, shift, n_out, tn=512):
    m, k = x.shape
    tm = ROW_TILE
    assert m % tm == 0 and aligned % tn == 0 and n_out % tn == 0 and 0 < shift < LANE
    n_plain = aligned // tn
    n_main = n_out // tn
    sub = tn // LANE
    return pl.pallas_call(
        functools.partial(_ab_in_kernel, n_plain=n_plain, n_main=n_main, shift=shift),
        out_shape=jax.ShapeDtypeStruct((m, n_out + tn), F32),
        grid=(m // tm, n_main + 1),
        in_specs=[pl.BlockSpec((tm, k), lambda i, j: (i, 0), pipeline_mode=pl.Buffered(1)),
                  pl.BlockSpec((1, k), lambda i, j: (0, 0)),
                  pl.BlockSpec((None, k, tn), lambda i, j: (layer, 0, jnp.minimum(j, n_main - 1))),
                  pl.BlockSpec((None, k, LANE), lambda i, j: (layer, 0, sub * (jnp.clip(j, n_plain, n_main - 1) + 1))),
                  pl.BlockSpec((k, LANE), lambda i, j: (0, 0), pipeline_mode=pl.Buffered(1))],
        out_specs=pl.BlockSpec((tm, tn), lambda i, j: (i, j)),
        scratch_shapes=[pltpu.VMEM((tm, k), BF16)],
        compiler_params=_cparams(("parallel", "arbitrary")),
        name="ab_in_proj",
    )(x, gain.reshape(1, k).astype(F32), w_all, w_all, w_small)


def _rmsnorm_kernel(x_ref, g_ref, o_ref):
    x = x_ref[...]
    o_ref[...] = x * lax.rsqrt(jnp.mean(x * x, axis=-1, keepdims=True) + RMS_EPS) * g_ref[...]


def _rmsnorm_rows(x, gain, *, row0, n_rows, rows):
    d = x.shape[1]
    assert n_rows % rows == 0 and row0 % rows == 0
    return pl.pallas_call(
        _rmsnorm_kernel,
        out_shape=jax.ShapeDtypeStruct((n_rows, d), F32),
        grid=(n_rows // rows,),
        in_specs=[pl.BlockSpec((rows, d), lambda i: (row0 // rows + i, 0)), pl.BlockSpec((1, d), lambda i: (0, 0))],
        out_specs=pl.BlockSpec((rows, d), lambda i: (i, 0)),
        compiler_params=_cparams(("parallel",)),
        name="final_rmsnorm",
    )(x, gain.reshape(1, d))


def _gla_gate(sm, wg_ref, bg_ref):
    gp = jnp.dot(sm.astype(BF16), wg_ref[...].astype(BF16), preferred_element_type=F32) + bg_ref[...]
    return _log_sigmoid(gp) / GLA_GATE_TAU


def _gated_head_norm(o, nw_ref, gate):
    on = o * lax.rsqrt(jnp.mean(o * o, axis=-1, keepdims=True) + RMS_EPS) * nw_ref[...]
    return on * _silu(gate)


def _gla_prompt_kernel(q_ref, k_ref, v_ref, go_ref, sm_ref, wg_ref, bg_ref, nw_ref,
                       o_ref, st_ref, sT_ref):
    c = pl.program_id(2)
    dk = q_ref.shape[1]
    C = CHUNK
    sub = GLA_SUB

    @pl.when(c == 0)
    def _():
        sT_ref[...] = jnp.zeros_like(sT_ref)

    r_i = lax.broadcasted_iota(jnp.int32, (C, C), 0)
    c_i = lax.broadcasted_iota(jnp.int32, (C, C), 1)
    tril = (r_i >= c_i).astype(F32)
    key_row = lax.broadcasted_iota(jnp.int32, (C, 1), 0)
    lane = lax.broadcasted_iota(jnp.int32, (sub, C), 1)
    rloc = lax.broadcasted_iota(jnp.int32, (sub, C), 0)

    for cc in range(q_ref.shape[0] // C):
        rows = slice(cc * C, (cc + 1) * C)
        q = q_ref[rows, :] * (dk ** -0.5)
        k = k_ref[rows, :]
        v = v_ref[rows, :].astype(BF16)
        g = _gla_gate(sm_ref[rows, :], wg_ref, bg_ref)
        cum = _dotx(tril, g)
        blocks = []
        for i in range(C // sub):
            r0 = i * sub
            qi = q[r0:r0 + sub]
            cumi = cum[r0:r0 + sub]
            dg = jnp.zeros((sub, C), F32)
            for s in range(r0, r0 + sub):
                x = qi * k[s:s + 1] * jnp.exp(jnp.minimum(cumi - cum[s:s + 1], 0.0))
                dg = jnp.where(lane == s, jnp.sum(x, axis=1, keepdims=True), dg)
            att_i = jnp.where(rloc + r0 >= lane, dg, 0.0)
            if i > 0:
                ci = cum[r0 - 1:r0]
                qt = qi * jnp.exp(cumi - ci)
                kt = jnp.where(key_row < r0, k * jnp.exp(jnp.minimum(ci - cum, 0.0)), 0.0)
                att_i = att_i + _nt(qt.astype(BF16), kt.astype(BF16))
            blocks.append(att_i)
        att = jnp.concatenate(blocks, axis=0)

        sT = sT_ref[...]
        last = cum[C - 1:C]
        o = jnp.dot(att.astype(BF16), v, preferred_element_type=F32)
        o = o + _nt((q * jnp.exp(cum)).astype(BF16), sT.astype(BF16))
        kd = k * jnp.exp(last - cum)
        sT_ref[...] = sT * jnp.exp(last) + _tn(v, kd.astype(BF16))
        o_ref[rows, :] = _gated_head_norm(o, nw_ref, go_ref[rows, :]).astype(o_ref.dtype)

    @pl.when(c == pl.num_programs(2) - 1)
    def _():
        st_ref[0, 0] = sT_ref[...].T


def _gla_prompt(proj, small, cols, wg_pad, b_gate, gla_norm, *, batch, seq, dk, dv, mix_shape):
    step = GLA_STEP_CHUNKS * CHUNK
    nc = seq // step
    assert seq % step == 0
    h = GLA_HEADS
    qb, kb, vb, gb = cols["gq"] // dk, cols["gk"] // dk, cols["gv"] // dv, cols["g_out"] // dv
    row = lambda b, hh, c: b * nc + c
    in_specs = [
        pl.BlockSpec((step, dk), lambda b, hh, c: (row(b, hh, c), qb + hh)),
        pl.BlockSpec((step, dk), lambda b, hh, c: (row(b, hh, c), kb + hh)),
        pl.BlockSpec((step, dv), lambda b, hh, c: (row(b, hh, c), vb + hh)),
        pl.BlockSpec((step, dv), lambda b, hh, c: (row(b, hh, c), gb + hh)),
        pl.BlockSpec((step, LANE), lambda b, hh, c: (row(b, hh, c), cols["small_blk"])),
        pl.BlockSpec((LANE, dk), lambda b, hh, c: (0, hh)),
        pl.BlockSpec((1, dk), lambda b, hh, c: (0, hh)),
        pl.BlockSpec((1, dv), lambda b, hh, c: (0, 0)),
    ]
    return pl.pallas_call(
        _gla_prompt_kernel,
        out_shape=(jax.ShapeDtypeStruct(mix_shape, BF16),
                   jax.ShapeDtypeStruct((batch, h, dk, dv), F32)),
        grid=(batch, h, nc),
        in_specs=in_specs,
        out_specs=(pl.BlockSpec((step, dv), lambda b, hh, c: (row(b, hh, c), hh)),
                   pl.BlockSpec((1, 1, dk, dv), lambda b, hh, c: (b, hh, 0, 0))),
        scratch_shapes=[pltpu.VMEM((dv, dk), F32)],
        compiler_params=_cparams(("parallel", "parallel", "arbitrary")),
        name="gla_prompt",
    )(proj, proj, proj, proj, small, wg_pad, b_gate.reshape(1, -1), gla_norm.reshape(1, -1))


def _alias_outputs(in_specs, args, bufs):
    aliases = {}
    for k, buf in enumerate(bufs):
        if buf is not None:
            aliases[len(args)] = k
            in_specs.append(pl.BlockSpec(memory_space=pl.ANY))
            args.append(buf)
    return aliases


def _gla_sample_kernel(q_ref, k_ref, v_ref, go_ref, sm_ref, wg_ref, bg_ref, nw_ref, s_ref, *rest):
    o_ref, so_ref = rest[-2:]
    nb, dk = q_ref.shape
    qT = (q_ref[...] * (dk ** -0.5)).T
    kT = k_ref[...].T
    egT = jnp.exp(_gla_gate(sm_ref[...], wg_ref, bg_ref)).T
    v = v_ref[...]
    rows = []
    for b in range(nb):
        s_new = s_ref[b, 0] * egT[:, b:b + 1] + kT[:, b:b + 1] * v[b:b + 1]
        so_ref[b, 0] = s_new
        rows.append(jnp.sum(qT[:, b:b + 1] * s_new, axis=0, keepdims=True))
    o = jnp.concatenate(rows, axis=0)
    o_ref[...] = _gated_head_norm(o, nw_ref, go_ref[...]).astype(o_ref.dtype)


def _gla_sample(proj, small, cols, wg_pad, b_gate, gla_norm, states, j, prev, mix, *, row0, dk, dv):
    nb = SAMPLE_ROWS
    bs, h = states.shape[1:3]
    assert bs % nb == 0 and row0 % nb == 0
    r0 = row0 // nb
    qb, kb, vb, gb = cols["gq"] // dk, cols["gk"] // dk, cols["gv"] // dv, cols["g_out"] // dv
    in_specs = [
        pl.BlockSpec((nb, dk), lambda i, hh: (r0 + i, qb + hh)),
        pl.BlockSpec((nb, dk), lambda i, hh: (r0 + i, kb + hh)),
        pl.BlockSpec((nb, dv), lambda i, hh: (r0 + i, vb + hh)),
        pl.BlockSpec((nb, dv), lambda i, hh: (r0 + i, gb + hh)),
        pl.BlockSpec((nb, LANE), lambda i, hh: (r0 + i, cols["small_blk"])),
        pl.BlockSpec((LANE, dk), lambda i, hh: (0, hh)),
        pl.BlockSpec((1, dk), lambda i, hh: (0, hh)),
        pl.BlockSpec((1, dv), lambda i, hh: (0, 0)),
        pl.BlockSpec((None, nb, 1, dk, dv), lambda i, hh: (j, i, hh, 0, 0)),
    ]
    args = [proj, proj, proj, proj, small, wg_pad, b_gate.reshape(1, -1), gla_norm.reshape(1, -1), states]
    aliases = _alias_outputs(in_specs, args, [mix, prev])
    return pl.pallas_call(
        _gla_sample_kernel,
        out_shape=(jax.ShapeDtypeStruct(mix.shape, BF16),
                   jax.ShapeDtypeStruct(states.shape, F32)),
        grid=(bs // nb, h),
        in_specs=in_specs,
        out_specs=(pl.BlockSpec((nb, dv), lambda i, hh: (r0 + i, hh)),
                   pl.BlockSpec((None, nb, 1, dk, dv), lambda i, hh: (j, i, hh, 0, 0))),
        input_output_aliases=aliases,
        compiler_params=_cparams(("parallel", "parallel")),
        name="gla_sample",
    )(*args)


def _dn_gates(sm, alog_ref, dtb_ref):
    decay = -jnp.exp(alog_ref[...]) * _softplus(sm + dtb_ref[...])
    return decay, jax.nn.sigmoid(sm)


def _lane_col(x, lane_idx):
    lane = lax.broadcasted_iota(jnp.int32, x.shape, 1)
    return jnp.sum(jnp.where(lane == lane_idx, x, 0.0), axis=1, keepdims=True)


def _l2norm(x):
    return x * lax.rsqrt(jnp.sum(x * x, axis=-1, keepdims=True) + L2_EPS)


def _split_bf16(x):
    hi = x.astype(BF16)
    return hi, (x - hi.astype(F32)).astype(BF16)


def _bmm3(a, b):
    ah, al = _split_bf16(a)
    bh, bl = _split_bf16(b)
    mm = lambda x, y: jnp.einsum('gij,gjk->gik', x, y, preferred_element_type=F32)
    return mm(ah, bh) + (mm(ah, bl) + mm(al, bh))


def _unit_lower_inverse(a, sub):
    C = a.shape[-1]
    assert C // sub == 4
    r = lax.broadcasted_iota(jnp.int32, (C, C), 0)
    c = lax.broadcasted_iota(jnp.int32, (C, C), 1)
    eye = (r == c).astype(F32)
    same_blk = (r // sub) == (c // sub)
    x = jnp.where(same_blk, -a, 0.0)
    low = jnp.where(same_blk, 0.0, a)
    td = eye + x
    p = x
    for _ in range(int(math.log2(sub)) - 1):
        p = _bmm3(p, p)
        td = _bmm3(td, eye + p)
    m = _bmm3(td, low)
    tm = _bmm3(eye - m, eye + _bmm3(m, m))
    return _bmm3(tm, td)


def _heads(x, n):
    return jnp.stack([x[:, h * DN_HEAD:(h + 1) * DN_HEAD] for h in range(n)], axis=0)


def _unheads(x):
    return jnp.concatenate([x[h] for h in range(x.shape[0])], axis=1)


def _dn_prep_kernel(q_ref, k_ref, v_ref, sm_ref, wq_ref, wk_ref, wv_ref, alog_ref, dtb_ref,
                    u_ref, w_ref, qe_ref, kd_ref, att_ref, cum_ref, cq_ref, ck_ref, cv_ref, prev_ref,
                    *, a_lane, b_lane):
    hg = pl.program_id(1)
    c = pl.program_id(2)
    C = q_ref.shape[0]
    d = DN_HEAD
    G = q_ref.shape[1] // d

    @pl.when(c == 0)
    def _():
        prev_ref[...] = jnp.zeros_like(prev_ref)

    def conv(x_ref, w_ref_, slot):
        x = x_ref[...]
        ext = jnp.concatenate([prev_ref[slot], x], axis=0)
        y = x * w_ref_[CONV_WIDTH - 1:CONV_WIDTH]
        for i in range(1, CONV_WIDTH):
            y = y + pltpu.roll(ext, i, 0)[8:8 + C] * w_ref_[CONV_WIDTH - 1 - i:CONV_WIDTH - i]
        prev_ref[slot] = x[C - 8:C]
        return _silu(y)

    q3 = _heads(conv(q_ref, wq_ref, 0), G)
    k3 = _heads(conv(k_ref, wk_ref, 1), G)
    v2 = _heads(conv(v_ref, wv_ref, 2), G)
    q2 = _l2norm(q3) * (d ** -0.5)
    k2 = _l2norm(k3)

    decay, beta_all = _dn_gates(sm_ref[...], alog_ref, dtb_ref)
    r_i = lax.broadcasted_iota(jnp.int32, (C, C), 0)
    c_i = lax.broadcasted_iota(jnp.int32, (C, C), 1)
    cum_all = _dotx((r_i >= c_i).astype(F32), decay)
    cum_ref[...] = cum_all
    r_w = lax.broadcasted_iota(jnp.int32, (C, LANE), 0)
    c_w = lax.broadcasted_iota(jnp.int32, (C, LANE), 1)
    cums, betas, cum_rows = [], [], []
    for hl in range(G):
        head = hg * G + hl
        cum_h = _lane_col(cum_all, a_lane + head)
        cums.append(cum_h)
        betas.append(_lane_col(beta_all, b_lane + head))
        cum_rows.append(jnp.sum(jnp.where(r_w == c_w, cum_h, 0.0), axis=0, keepdims=True))
    cum = jnp.stack(cums, axis=0)
    beta = jnp.stack(betas, axis=0)
    cum_row = jnp.stack(cum_rows, axis=0)
    dec_w = jnp.exp(jnp.minimum(cum - cum_row, 0.0))
    dec = dec_w[:, :, :C]

    kbeta = k2 * beta
    k2b = k2.astype(BF16)
    kk = jnp.einsum('gtd,gsd->gts', kbeta.astype(BF16), k2b, preferred_element_type=F32)
    t = _unit_lower_inverse(jnp.where(r_i > c_i, kk * dec, 0.0), GLA_SUB)
    ecum = jnp.exp(cum)
    sol = _bmm3(t, jnp.concatenate([v2 * beta, kbeta * ecum], axis=-1))
    k_wide = jnp.concatenate([k2b, jnp.zeros_like(k2b)], axis=1)
    qk = jnp.einsum('gtd,gsd->gts', q2.astype(BF16), k_wide, preferred_element_type=F32)
    att = jnp.where(r_w >= c_w, qk * dec_w, 0.0)
    last = cum[:, C - 1:C]

    u_ref[...] = _unheads(sol[:, :, :d])
    w_ref[...] = _unheads(sol[:, :, d:]).astype(BF16)
    qe_ref[...] = _unheads(q2 * ecum).astype(BF16)
    kd_ref[...] = _unheads(k2 * jnp.exp(last - cum)).astype(BF16)
    att_ref[...] = _unheads(att).astype(BF16)

    @pl.when(c == pl.num_programs(2) - 1)
    def _():
        cq_ref[0] = prev_ref[0]
        ck_ref[0] = prev_ref[1]
        cv_ref[0] = prev_ref[2]


def _dn_scan_kernel(u_ref, w_ref, qe_ref, kd_ref, att_ref, cum_ref, z_ref, nw_ref, *rest, a_lane, group):
    o_ref, st_ref, s_ref = rest[-3:]
    c = pl.program_id(1)
    C = u_ref.shape[0]
    H = DN_HEADS

    @pl.when(c == 0)
    def _():
        s_ref[...] = jnp.zeros_like(s_ref)

    s = s_ref[...]
    sb = s.astype(BF16)
    v_new = _heads(u_ref[...], H) - jnp.einsum('hcd,hdv->hcv', _heads(w_ref[...], H), sb,
                                               preferred_element_type=F32)
    vb = v_new.astype(BF16)
    o = jnp.einsum('hts,hsv->htv', _heads(att_ref[...], H)[:, :, :C], vb, preferred_element_type=F32)
    o = o + jnp.einsum('hcd,hdv->hcv', _heads(qe_ref[...], H), sb, preferred_element_type=F32)
    cum_last = cum_ref[C - 1:C, :]
    lanes = [(h // group) * LANE + a_lane + h for h in range(H)]
    e_last = jnp.exp(jnp.stack([cum_last[:, l:l + 1] for l in lanes], axis=0))
    s_new = s * e_last + jnp.einsum('hcd,hcv->hdv', _heads(kd_ref[...], H), vb, preferred_element_type=F32)
    s_ref[...] = s_new
    on = o * lax.rsqrt(jnp.mean(o * o, axis=-1, keepdims=True) + RMS_EPS) * nw_ref[...]
    o_ref[...] = _unheads(on * _silu(_heads(z_ref[...], H))).astype(o_ref.dtype)

    @pl.when(c == pl.num_programs(1) - 1)
    def _():
        st_ref[0] = s_new


def _dn_prompt(proj, small, cols, conv_w, j, alog_pad, dtb_pad, dn_norm, mix, *, batch, seq):
    nc = seq // CHUNK
    assert seq % CHUNK == 0
    G, d = DN_GROUP, DN_HEAD
    gw = G * d
    ng = DN_HEADS // G
    dn_k = DN_HEADS * d
    t = batch * seq
    qb, kb, vb = cols["dn_q"] // gw, cols["dn_k"] // gw, cols["dn_v"] // gw
    row = lambda b, g, c: b * nc + c
    wide = lambda dt: jax.ShapeDtypeStruct((t, dn_k), dt)
    tok = pl.BlockSpec((CHUNK, gw), lambda b, g, c: (row(b, g, c), g))
    tail = pl.BlockSpec((1, 8, gw), lambda b, g, c: (b, 0, g))
    u, w, qe, kd, att, cum, cq, ck, cv = pl.pallas_call(
        functools.partial(_dn_prep_kernel, a_lane=cols["a_lane"], b_lane=cols["b_lane"]),
        out_shape=(wide(F32), wide(BF16), wide(BF16), wide(BF16), wide(BF16),
                   jax.ShapeDtypeStruct((t, ng * LANE), F32),
                   jax.ShapeDtypeStruct((batch, 8, dn_k), F32), jax.ShapeDtypeStruct((batch, 8, dn_k), F32),
                   jax.ShapeDtypeStruct((batch, 8, dn_k), F32)),
        grid=(batch, ng, nc),
        in_specs=[
            pl.BlockSpec((CHUNK, gw), lambda b, g, c: (row(b, g, c), qb + g)),
            pl.BlockSpec((CHUNK, gw), lambda b, g, c: (row(b, g, c), kb + g)),
            pl.BlockSpec((CHUNK, gw), lambda b, g, c: (row(b, g, c), vb + g)),
            pl.BlockSpec((CHUNK, LANE), lambda b, g, c: (row(b, g, c), cols["small_blk"])),
            pl.BlockSpec((None, CONV_WIDTH, gw), lambda b, g, c: (j, 0, g)),
            pl.BlockSpec((None, CONV_WIDTH, gw), lambda b, g, c: (j, 0, dn_k // gw + g)),
            pl.BlockSpec((None, CONV_WIDTH, gw), lambda b, g, c: (j, 0, 2 * dn_k // gw + g)),
            pl.BlockSpec((1, LANE), lambda b, g, c: (0, 0)),
            pl.BlockSpec((1, LANE), lambda b, g, c: (0, 0)),
        ],
        out_specs=(tok, tok, tok, tok, tok,
                   pl.BlockSpec((CHUNK, LANE), lambda b, g, c: (row(b, g, c), g)), tail, tail, tail),
        scratch_shapes=[pltpu.VMEM((3, 8, gw), F32)],
        compiler_params=_cparams(("parallel", "parallel", "arbitrary")),
        name="deltanet_prep",
    )(proj, proj, proj, small, conv_w, conv_w, conv_w, alog_pad, dtb_pad)

    full = lambda width: pl.BlockSpec((CHUNK, width), lambda b, c: (b * nc + c, 0))
    in_specs = [full(dn_k), full(dn_k), full(dn_k), full(dn_k), full(dn_k), full(ng * LANE),
                pl.BlockSpec((CHUNK, dn_k), lambda b, c: (b * nc + c, cols["dn_z"] // dn_k)),
                pl.BlockSpec((1, d), lambda b, c: (0, 0))]
    args = [u, w, qe, kd, att, cum, proj, dn_norm.reshape(1, -1)]
    aliases = _alias_outputs(in_specs, args, [mix])
    mix, st = pl.pallas_call(
        functools.partial(_dn_scan_kernel, a_lane=cols["a_lane"], group=G),
        out_shape=(jax.ShapeDtypeStruct(mix.shape, BF16),
                   jax.ShapeDtypeStruct((batch, DN_HEADS, d, d), F32)),
        grid=(batch, nc),
        in_specs=in_specs,
        out_specs=(pl.BlockSpec((CHUNK, dn_k), lambda b, c: (b * nc + c, mix.shape[1] // dn_k - 1)),
                   pl.BlockSpec((1, DN_HEADS, d, d), lambda b, c: (b, 0, 0, 0))),
        scratch_shapes=[pltpu.VMEM((DN_HEADS, d, d), F32)],
        input_output_aliases=aliases,
        compiler_params=_cparams(("parallel", "arbitrary")),
        name="deltanet_scan",
    )(*args)
    tail_rows = slice(8 - (CONV_WIDTH - 1), 8)
    conv_new = jnp.concatenate([cq[:, tail_rows], ck[:, tail_rows], cv[:, tail_rows]], axis=-1)
    return mix, st, conv_new


def _dn_sample_kernel(q_ref, k_ref, v_ref, z_ref, sm_ref, bq_ref, bk_ref, bv_ref, wq_ref, wk_ref, wv_ref,
                      alog_ref, dtb_ref, nw_ref, s_ref, *rest, a_lane, b_lane):
    o_ref, so_ref = rest[-2:]
    hg = pl.program_id(1)
    nb = q_ref.shape[0]
    G = s_ref.shape[1]
    d = DN_HEAD

    def conv(x_ref, buf_ref, w_ref):
        y = x_ref[...] * w_ref[CONV_WIDTH - 1:CONV_WIDTH]
        for i in range(CONV_WIDTH - 1):
            y = y + buf_ref[:, i, :] * w_ref[i:i + 1]
        return _silu(y)

    yq = conv(q_ref, bq_ref, wq_ref)
    yk = conv(k_ref, bk_ref, wk_ref)
    yv = conv(v_ref, bv_ref, wv_ref)
    decay, beta_all = _dn_gates(sm_ref[...], alog_ref, dtb_ref)
    z = z_ref[...]
    outs = []
    for hl in range(G):
        head = hg * G + hl
        sl = slice(hl * d, (hl + 1) * d)
        q2 = _l2norm(yq[:, sl]) * (d ** -0.5)
        k2 = _l2norm(yk[:, sl])
        v2 = yv[:, sl]
        eg = jnp.exp(_lane_col(decay, a_lane + head))
        beta = _lane_col(beta_all, b_lane + head)
        qT = q2.T
        kT = k2.T
        rows = []
        for b in range(nb):
            s = s_ref[b, hl]
            kc = kT[:, b:b + 1]
            ks = jnp.sum(kc * s, axis=0, keepdims=True)
            v_new = beta[b:b + 1] * (v2[b:b + 1] - eg[b:b + 1] * ks)
            s_new = s * eg[b:b + 1] + kc * v_new
            so_ref[b, hl] = s_new
            rows.append(jnp.sum(qT[:, b:b + 1] * s_new, axis=0, keepdims=True))
        outs.append(_gated_head_norm(jnp.concatenate(rows, axis=0), nw_ref, z[:, sl]))
    o_ref[...] = jnp.concatenate(outs, axis=1).astype(o_ref.dtype)


def _dn_sample(proj, small, cols, conv_w, j, alog_pad, dtb_pad, dn_norm, states, conv_bufs, prev, mix, *, row0):
    nb = SAMPLE_ROWS
    bs = states.shape[1]
    G, d = DN_GROUP, DN_HEAD
    gw = G * d
    ng = DN_HEADS // G
    dn_k = DN_HEADS * d
    assert bs % nb == 0 and row0 % nb == 0
    r0 = row0 // nb
    qb, kb, vb, zb = cols["dn_q"] // gw, cols["dn_k"] // gw, cols["dn_v"] // gw, cols["dn_z"] // gw
    ob = (mix.shape[1] - dn_k) // gw
    nbuf = CONV_WIDTH - 1
    in_specs = [
        pl.BlockSpec((nb, gw), lambda i, g: (r0 + i, qb + g)),
        pl.BlockSpec((nb, gw), lambda i, g: (r0 + i, kb + g)),
        pl.BlockSpec((nb, gw), lambda i, g: (r0 + i, vb + g)),
        pl.BlockSpec((nb, gw), lambda i, g: (r0 + i, zb + g)),
        pl.BlockSpec((nb, LANE), lambda i, g: (r0 + i, cols["small_blk"])),
        pl.BlockSpec((None, nb, nbuf, gw), lambda i, g: (j, i, 0, g)),
        pl.BlockSpec((None, nb, nbuf, gw), lambda i, g: (j, i, 0, dn_k // gw + g)),
        pl.BlockSpec((None, nb, nbuf, gw), lambda i, g: (j, i, 0, 2 * dn_k // gw + g)),
        pl.BlockSpec((None, CONV_WIDTH, gw), lambda i, g: (j, 0, g)),
        pl.BlockSpec((None, CONV_WIDTH, gw), lambda i, g: (j, 0, dn_k // gw + g)),
        pl.BlockSpec((None, CONV_WIDTH, gw), lambda i, g: (j, 0, 2 * dn_k // gw + g)),
        pl.BlockSpec((1, LANE), lambda i, g: (0, 0)),
        pl.BlockSpec((1, LANE), lambda i, g: (0, 0)),
        pl.BlockSpec((1, d), lambda i, g: (0, 0)),
        pl.BlockSpec((None, nb, G, d, d), lambda i, g: (j, i, g, 0, 0)),
    ]
    args = [proj, proj, proj, proj, small, conv_bufs, conv_bufs, conv_bufs, conv_w, conv_w, conv_w,
            alog_pad, dtb_pad, dn_norm.reshape(1, -1), states]
    aliases = _alias_outputs(in_specs, args, [mix, prev])
    return pl.pallas_call(
        functools.partial(_dn_sample_kernel, a_lane=cols["a_lane"], b_lane=cols["b_lane"]),
        out_shape=(jax.ShapeDtypeStruct(mix.shape, BF16),
                   jax.ShapeDtypeStruct(states.shape, F32)),
        grid=(bs // nb, ng),
        in_specs=in_specs,
        out_specs=(pl.BlockSpec((nb, gw), lambda i, g: (r0 + i, ob + g)),
                   pl.BlockSpec((None, nb, G, d, d), lambda i, g: (j, i, g, 0, 0))),
        input_output_aliases=aliases,
        compiler_params=_cparams(("parallel", "parallel")),
        name="deltanet_sample",
    )(*args)


def _mla_post_kernel(ckv_ref, kpe_ref, cs_ref, g_ref, lat_ref, kr_ref):
    x = ckv_ref[...]
    lat_ref[...] = x * lax.rsqrt(jnp.mean(x * x, axis=-1, keepdims=True) + RMS_EPS) * g_ref[...]
    zz = kpe_ref[...] * cs_ref[...]
    kr_ref[...] = zz + pltpu.roll(zz, QK_ROPE, 1)


def _mla_post(a, cs, kv_norm, *, rows=520):
    m = a.shape[0]
    assert m % rows == 0
    ckv_blk = Q_LORA // KV_LORA
    kpe_blk = (Q_LORA + KV_LORA) // LANE
    return pl.pallas_call(
        _mla_post_kernel,
        out_shape=(jax.ShapeDtypeStruct((m, KV_LORA), F32), jax.ShapeDtypeStruct((m, LANE), F32)),
        grid=(m // rows,),
        in_specs=[pl.BlockSpec((rows, KV_LORA), lambda i: (i, ckv_blk)),
                  pl.BlockSpec((rows, LANE), lambda i: (i, kpe_blk)),
                  pl.BlockSpec((rows, LANE), lambda i: (i, 0)),
                  pl.BlockSpec((1, KV_LORA), lambda i: (0, 0))],
        out_specs=(pl.BlockSpec((rows, KV_LORA), lambda i: (i, 0)),
                   pl.BlockSpec((rows, LANE), lambda i: (i, 0))),
        compiler_params=_cparams(("parallel",)),
        name="mla_latent_rope",
    )(a, a, cs, kv_norm.reshape(1, -1))


def _mla_prefill_kernel(qn_ref, qr_ref, cs_ref, kn_ref, v_ref, kr_ref, o_ref, *, scale, tile):
    n = qn_ref.shape[0] // tile
    kf = jnp.concatenate([kn_ref[...], kr_ref[...].astype(BF16)], axis=1)
    qloc = lax.broadcasted_iota(jnp.int32, (tile, tile), 0)
    kloc = lax.broadcasted_iota(jnp.int32, (tile, tile), 1)
    for qi in range(n):
        rows = slice(qi * tile, (qi + 1) * tile)
        q = jnp.concatenate([(qn_ref[rows, :] * scale).astype(BF16),
                             (qr_ref[rows, :] * cs_ref[rows, :] * scale).astype(BF16)], axis=1)
        for kj in range(qi + 1):
            ks = slice(kj * tile, (kj + 1) * tile)
            s = _nt(q, kf[ks])
            if kj == qi:
                s = jnp.where(kloc <= qloc, s, -jnp.inf)
            s_max = jnp.max(s, axis=-1, keepdims=True)
            if kj == 0:
                m = s_max
                p = jnp.exp(s - m)
                l = jnp.sum(p, axis=-1, keepdims=True)
                acc = jnp.dot(p.astype(BF16), v_ref[ks, :], preferred_element_type=F32)
            else:
                m_new = jnp.maximum(m, s_max)
                corr = jnp.exp(m - m_new)
                p = jnp.exp(s - m_new)
                l = l * corr + jnp.sum(p, axis=-1, keepdims=True)
                acc = acc * corr + jnp.dot(p.astype(BF16), v_ref[ks, :], preferred_element_type=F32)
                m = m_new
        o_ref[rows, :] = (acc / l).astype(o_ref.dtype)


def _mla_prefill(q, cs, kv, kr2, *, batch, seq, out_rows):
    assert seq % ATT_TQ == 0
    h = MLA_HEADS
    scale = (QK_NOPE + QK_ROPE) ** -0.5
    return pl.pallas_call(
        functools.partial(_mla_prefill_kernel, scale=scale, tile=ATT_TQ),
        out_shape=jax.ShapeDtypeStruct((out_rows, h * V_HEAD), BF16),
        grid=(batch, h),
        in_specs=[
            pl.BlockSpec((seq, QK_NOPE), lambda b, hh: (b, hh)),
            pl.BlockSpec((seq, LANE), lambda b, hh: (b, h + hh)),
            pl.BlockSpec((seq, LANE), lambda b, hh: (b, 0)),
            pl.BlockSpec((seq, QK_NOPE), lambda b, hh: (b, 2 * hh)),
            pl.BlockSpec((seq, V_HEAD), lambda b, hh: (b, 2 * hh + 1)),
            pl.BlockSpec((seq, LANE), lambda b, hh: (b, 0)),
        ],
        out_specs=pl.BlockSpec((seq, V_HEAD), lambda b, hh: (b, hh)),
        compiler_params=_cparams(("parallel", "parallel")),
        name="mla_prefill",
    )(q, q, cs, kv, kv, kr2)


def _absorb_q_kernel(x_ref, w_ref, o_ref):
    o_ref[...] = _nt(x_ref[...].astype(BF16), w_ref[...].astype(BF16)).astype(o_ref.dtype)


def _absorb_q(q, w_kvb, j, *, row0, rows):
    h = MLA_HEADS
    assert row0 % rows == 0
    return pl.pallas_call(
        _absorb_q_kernel,
        out_shape=jax.ShapeDtypeStruct((rows, h * KV_LORA), BF16),
        grid=(h,),
        in_specs=[pl.BlockSpec((rows, QK_NOPE), lambda hh: (row0 // rows, hh)),
                  pl.BlockSpec((None, KV_LORA, QK_NOPE), lambda hh: (j, 0, 2 * hh))],
        out_specs=pl.BlockSpec((rows, KV_LORA), lambda hh: (0, hh)),
        compiler_params=_cparams(("parallel",)),
        name="mla_absorb_q",
    )(q, w_kvb)


def _absorb_v_kernel(x_ref, w_ref, *rest):
    o_ref = rest[-1]
    o_ref[...] = jnp.dot(x_ref[...].astype(BF16), w_ref[...].astype(BF16),
                         preferred_element_type=F32).astype(o_ref.dtype)


def _absorb_v(o_lat, w_kvb, j, att, *, row0):
    rows = o_lat.shape[0]
    h = MLA_HEADS
    assert row0 % rows == 0
    in_specs = [pl.BlockSpec((rows, KV_LORA), lambda hh: (0, hh)),
                pl.BlockSpec((None, KV_LORA, V_HEAD), lambda hh: (j, 0, 2 * hh + 1))]
    args = [o_lat, w_kvb]
    aliases = _alias_outputs(in_specs, args, [att])
    return pl.pallas_call(
        _absorb_v_kernel,
        out_shape=jax.ShapeDtypeStruct(att.shape, BF16),
        grid=(h,),
        in_specs=in_specs,
        out_specs=pl.BlockSpec((rows, V_HEAD), lambda hh: (row0 // rows, hh)),
        input_output_aliases=aliases,
        compiler_params=_cparams(("parallel",)),
        name="mla_absorb_v",
    )(*args)


def _mla_decode_kernel(pt_ref, ql_ref, qr_ref, cs_ref, cn_ref, rn_ref, lat_hbm, rope_hbm, o_ref,
                       lat_buf, rope_buf, sem, m_ref, l_ref, acc_ref, *, scale, n_pages, layer):
    g = pl.program_id(1)
    n_groups = pl.num_programs(1)
    t = pl.program_id(0) * n_groups + g
    slot = t % 2

    def page_copies(step, sl):
        out = []
        for i in range(n_pages):
            pid = pt_ref[step * n_pages + i]
            out.append(pltpu.make_async_copy(lat_hbm.at[layer, pid], lat_buf.at[sl, i], sem.at[sl, 0]))
            out.append(pltpu.make_async_copy(rope_hbm.at[layer, pid], rope_buf.at[sl, i], sem.at[sl, 1]))
        return out

    @pl.when(t == 0)
    def _():
        for cp in page_copies(t, slot):
            cp.start()

    @pl.when(t + 1 < pl.num_programs(0) * n_groups)
    def _():
        for cp in page_copies(t + 1, 1 - slot):
            cp.start()

    for cp in page_copies(t, slot):
        cp.wait()
    lat_refs = [lat_buf.at[slot, i] for i in range(n_pages)]
    rope_refs = [rope_buf.at[slot, i] for i in range(n_pages)]
    ql = ql_ref[0]
    zz = qr_ref[0] * cs_ref[...]

    @pl.when(g == 0)
    def _():
        s0 = (jnp.sum(ql.astype(F32) * cn_ref[0], axis=-1, keepdims=True)
              + jnp.sum(zz * rn_ref[0], axis=-1, keepdims=True)) * scale
        m_ref[...] = s0
        l_ref[...] = jnp.ones_like(l_ref)
        acc_ref[...] = jnp.broadcast_to(cn_ref[0], acc_ref.shape)

    qrot = (zz + pltpu.roll(zz, QK_ROPE, 1))[:, :QK_ROPE].astype(BF16)
    cbs = [lat_refs[i][...].astype(BF16) for i in range(n_pages)]
    s = jnp.concatenate(
        [_nt(ql, cbs[i]) + jnp.dot(qrot, rope_refs[i][...].astype(BF16), preferred_element_type=F32)
         for i in range(n_pages)], axis=1) * scale
    m = m_ref[...]
    m_new = jnp.maximum(m, jnp.max(s, axis=-1, keepdims=True))
    corr = jnp.exp(m - m_new)
    e = jnp.exp(s - m_new)
    l_ref[...] = l_ref[...] * corr + jnp.sum(e, axis=-1, keepdims=True)
    eb = e.astype(BF16)
    page = cbs[0].shape[0]
    pv = jnp.dot(eb[:, :page], cbs[0], preferred_element_type=F32)
    for i in range(1, n_pages):
        pv = pv + jnp.dot(eb[:, i * page:(i + 1) * page], cbs[i], preferred_element_type=F32)
    acc_ref[...] = acc_ref[...] * corr + pv
    m_ref[...] = m_new

    @pl.when(g == pl.num_programs(1) - 1)
    def _():
        o_ref[0] = acc_ref[...] / l_ref[...]


def _mla_decode(q_lat, q_rope, cs_row, lat_new, kr_new, cache_latent, cache_k_rope_t, page_table, layer):
    bs = q_lat.shape[0]
    h = MLA_HEADS
    n_log = page_table.shape[1]
    page = cache_latent.shape[2]
    npg = DEC_PAGES
    assert n_log % npg == 0
    scale = (QK_NOPE + QK_ROPE) ** -0.5

    in_specs = [
        pl.BlockSpec((1, h, KV_LORA), lambda b, g, pt: (b, 0, 0)),
        pl.BlockSpec((1, h, LANE), lambda b, g, pt: (b, 0, 0)),
        pl.BlockSpec((1, LANE), lambda b, g, pt: (0, 0)),
        pl.BlockSpec((1, 1, KV_LORA), lambda b, g, pt: (b, 0, 0)),
        pl.BlockSpec((1, 1, LANE), lambda b, g, pt: (b, 0, 0)),
        pl.BlockSpec(memory_space=pl.ANY),
        pl.BlockSpec(memory_space=pl.ANY),
    ]
    grid_spec = pltpu.PrefetchScalarGridSpec(
        num_scalar_prefetch=1,
        grid=(bs, n_log // npg),
        in_specs=in_specs,
        out_specs=pl.BlockSpec((1, h, KV_LORA), lambda b, g, pt: (b, 0, 0)),
        scratch_shapes=[pltpu.VMEM((2, npg, page, KV_LORA), F32), pltpu.VMEM((2, npg, QK_ROPE, page), F32),
                        pltpu.SemaphoreType.DMA((2, 2)),
                        pltpu.VMEM((h, 1), F32), pltpu.VMEM((h, 1), F32), pltpu.VMEM((h, KV_LORA), F32)],
    )
    return pl.pallas_call(
        functools.partial(_mla_decode_kernel, scale=scale, n_pages=npg, layer=layer),
        out_shape=jax.ShapeDtypeStruct((bs, h, KV_LORA), F32),
        grid_spec=grid_spec,
        compiler_params=_cparams(("arbitrary", "arbitrary")),
        name="mla_decode",
    )(page_table.reshape(-1), q_lat.reshape(bs, h, KV_LORA), q_rope.reshape(bs, h, LANE), cs_row,
      lat_new.reshape(bs, 1, KV_LORA), kr_new.reshape(bs, 1, LANE), cache_latent, cache_k_rope_t)


def _rot_cols(w):
    half = QK_ROPE // 2
    return jnp.concatenate([-w[..., half:], w[..., :half]], axis=-1)


def _ab_columns(d_model):
    gla_k = GLA_HEADS * (d_model // 16)
    gla_v = GLA_HEADS * (d_model // 8)
    dn = DN_HEADS * DN_HEAD
    aligned = 2 * gla_k + gla_v
    n_main = aligned + gla_v + 4 * dn
    cols = dict(gq=0, gk=gla_k, gv=2 * gla_k, g_out=aligned, dn_q=aligned + gla_v, dn_k=aligned + gla_v + dn,
                dn_v=aligned + gla_v + 2 * dn, dn_z=aligned + gla_v + 3 * dn, small_blk=n_main // LANE,
                a_lane=GLA_GATE_RANK, b_lane=GLA_GATE_RANK + DN_HEADS)
    return cols, aligned, n_main


def _ab_small_weight(w_all, j, aligned, n_main):
    lo = w_all[j, :, aligned:aligned + GLA_GATE_RANK]
    hi = w_all[j, :, n_main + GLA_GATE_RANK:]
    pad = jnp.zeros((lo.shape[0], LANE - GLA_GATE_RANK - hi.shape[1]), lo.dtype)
    return jnp.concatenate([lo, hi, pad], axis=1)


def _pad_lanes(v, lane0):
    return jnp.zeros((1, LANE), F32).at[0, lane0:lane0 + v.shape[0]].set(v.astype(F32))


def _rope_table(pos):
    half = QK_ROPE // 2
    inv = ROPE_THETA ** (-jnp.arange(half, dtype=F32) / half)
    ang = pos.astype(F32)[:, None] * inv[None, :]
    cos, sin = jnp.cos(ang), jnp.sin(ang)
    return jnp.concatenate([cos, cos, sin, sin], axis=1)


def _mla_mixer(x, cs, norm_w, w_a, q_norm, kv_norm, w_qb, w_kvb, w_o, cache_latent, cache_k_rope_t,
               page_table, j, *, bp, lp, bs):
    tp = bp * lp
    d_model = x.shape[1]
    kpe_w = w_a[:, Q_LORA + KV_LORA:]
    a_pad = jnp.zeros((d_model, -(Q_LORA + KV_LORA + LANE) % 512), F32)
    w_a2 = jnp.concatenate([w_a[:, :Q_LORA + KV_LORA], kpe_w, _rot_cols(kpe_w), a_pad], axis=1)
    a = _matmul(x, [w_a2], tn=512, out_dtype=F32, gain=norm_w, name="mla_a_proj")
    latent, kr2 = _mla_post(a, cs, kv_norm)
    w_qb = w_qb.reshape(Q_LORA, MLA_HEADS, QK_NOPE + QK_ROPE)
    w_rope = w_qb[..., QK_NOPE:]
    w_q2 = jnp.concatenate(
        [w_qb[..., :QK_NOPE].reshape(Q_LORA, -1),
         jnp.concatenate([w_rope, _rot_cols(w_rope)], axis=-1).reshape(Q_LORA, -1)], axis=1)
    q = _matmul(a, [w_q2], tn=2048, out_dtype=F32, gain=q_norm, name="mla_q_proj")
    kv = _matmul(latent, [w_kvb], tn=2048, out_dtype=BF16, layer=j, name="mla_kv_proj")
    att = _mla_prefill(q, cs, kv, kr2, batch=bp, seq=lp, out_rows=x.shape[0])
    q_lat = _absorb_q(q, w_kvb, j, row0=tp, rows=bs)
    q_rope_s = q[tp:, MLA_HEADS * QK_NOPE:]
    o_lat = _mla_decode(q_lat, q_rope_s, cs[tp:tp + 1], latent[tp:], kr2[tp:],
                        cache_latent, cache_k_rope_t, page_table, j)
    att = _absorb_v(o_lat.reshape(bs, MLA_HEADS * KV_LORA), w_kvb, j, att, row0=tp)
    x = _matmul(att, [w_o], tn=512, out_dtype=F32, resid=x, layer=j, name="mla_out_proj")
    return x, latent, kr2


def kernel(x_prompt, x_sample, state_gla, state_delta, state_conv, cache_latent, cache_k_rope, page_table,
           norm_mix, norm_ffn, norm_final, ab_w_in, gla_w_gate, gla_b_gate, gla_norm, dn_conv_w,
           dn_a_log, dn_dt_bias, dn_norm, ab_w_out, mla_w_a, mla_q_norm, mla_kv_norm, mla_w_qb,
           mla_w_kvb, mla_w_o, ffn_w_gate, ffn_w_up, ffn_w_down):
    bp, lp, d_model = x_prompt.shape
    bs, ls, _ = x_sample.shape
    assert ls == 1
    tp = bp * lp
    depth = norm_mix.shape[0]
    past_len = page_table.shape[1] * cache_latent.shape[2]
    gla_dk, gla_dv = state_gla.shape[3:]
    dn_ch = state_conv.shape[-1]

    x = jnp.concatenate([x_prompt.reshape(tp, d_model), x_sample.reshape(bs, d_model)], axis=0)
    pos = jnp.concatenate([jnp.tile(jnp.arange(lp), bp), jnp.full((bs,), past_len)])
    cs = _rope_table(pos)

    cache_k_rope_t = jnp.swapaxes(cache_k_rope, 2, 3)
    gla_p, dn_p, cv_p, cv_s = [], [], [], []
    lat_p, lat_s, kr_p, kr_s = [], [], [], []
    gla_s = dn_s = None
    for layer in range(depth):
        j = layer // 2
        if layer % 2 == 0:
            cols, aligned, n_main = _ab_columns(d_model)
            proj = _ab_in_proj(x, norm_mix[layer], ab_w_in, j, _ab_small_weight(ab_w_in, j, aligned, n_main),
                               aligned=aligned, shift=GLA_GATE_RANK, n_out=n_main)
            small = proj
            wg_pad = jnp.zeros((LANE, gla_w_gate.shape[2]), F32).at[:GLA_GATE_RANK].set(gla_w_gate[j])
            alog_pad = _pad_lanes(dn_a_log[j], cols["a_lane"])
            dtb_pad = _pad_lanes(dn_dt_bias[j], cols["a_lane"])
            mix, g1 = _gla_prompt(proj, small, cols, wg_pad, gla_b_gate[j], gla_norm[j], batch=bp, seq=lp,
                                  dk=gla_dk, dv=gla_dv, mix_shape=(tp + bs, GLA_HEADS * gla_dv + DN_HEADS * DN_HEAD))
            mix, gla_s = _gla_sample(proj, small, cols, wg_pad, gla_b_gate[j], gla_norm[j], state_gla, j, gla_s, mix,
                                     row0=tp, dk=gla_dk, dv=gla_dv)
            mix, d1, c1 = _dn_prompt(proj, small, cols, dn_conv_w, j, alog_pad, dtb_pad, dn_norm[j], mix,
                                     batch=bp, seq=lp)
            mix, dn_s = _dn_sample(proj, small, cols, dn_conv_w, j, alog_pad, dtb_pad, dn_norm[j],
                                   state_delta, state_conv, dn_s, mix, row0=tp)
            c2 = jnp.concatenate([state_conv[j, :, 1:], proj[tp:, None, cols["dn_q"]:cols["dn_q"] + dn_ch]], axis=1)
            x = _matmul(mix, [ab_w_out], tn=512, out_dtype=F32, resid=x, layer=j, name="ab_out_proj")
            gla_p.append(g1)
            dn_p.append(d1)
            cv_p.append(c1); cv_s.append(c2)
        else:
            x, latent, kr2 = _mla_mixer(x, cs, norm_mix[layer], mla_w_a[j], mla_q_norm[j], mla_kv_norm[j],
                                        mla_w_qb[j], mla_w_kvb, mla_w_o, cache_latent, cache_k_rope_t,
                                        page_table, j, bp=bp, lp=lp, bs=bs)
            lat_p.append(latent[:tp].reshape(bp, lp, KV_LORA)); lat_s.append(latent[tp:].reshape(bs, 1, KV_LORA))
            kr_p.append(kr2[:tp, :QK_ROPE].reshape(bp, lp, QK_ROPE)); kr_s.append(kr2[tp:, :QK_ROPE].reshape(bs, 1, QK_ROPE))
        hid = _matmul(x, [ffn_w_gate, ffn_w_up], tn=256, out_dtype=BF16, gain=norm_ffn[layer], layer=layer,
                      name="ffn_gate_up")
        x = _matmul(hid, [ffn_w_down], tn=256, out_dtype=F32, resid=x, layer=layer, name="ffn_down")
    y_prompt = _rmsnorm_rows(x, norm_final, row0=0, n_rows=tp, rows=256).reshape(bp, lp, d_model)
    y_sample = _rmsnorm_rows(x, norm_final, row0=tp, n_rows=bs, rows=bs).reshape(bs, 1, d_model)
    return (y_prompt, y_sample, jnp.stack(gla_p), gla_s, jnp.stack(dn_p), dn_s,
            jnp.stack(cv_p), jnp.stack(cv_s), jnp.stack(lat_p), jnp.stack(lat_s), jnp.stack(kr_p), jnp.stack(kr_s))
```

```python
import functools
import math

import numpy as np
import jax
import jax.numpy as jnp
from jax import lax
from jax.experimental import pallas as pl
from jax.experimental.pallas import tpu as pltpu

F32 = jnp.float32
BF16 = jnp.bfloat16
HIGHEST = lax.Precision.HIGHEST

GLA_HEADS = 4
GLA_GATE_RANK = 16
GLA_GATE_TAU = 16.0
DN_HEADS = 16
DN_HEAD = 128
CONV_WIDTH = 4
CHUNK = 64
MLA_HEADS = 32
Q_LORA = 1024
KV_LORA = 512
QK_NOPE = 128
QK_ROPE = 64
V_HEAD = 128
ROPE_THETA = 10000.0
RMS_EPS = 1e-6
L2_EPS = 1e-6

LANE = 128
VMEM_LIMIT_BYTES = 60 * 1024 * 1024

ROW_TILE = 1040
NORM_ROWS = 16
GLA_SUB = 8
DN_SUB = 16
GLA_STEP_CHUNKS = 4
DN_GROUP = 8
ATT_TQ = 512
DEC_PAGES = 32
SAMPLE_ROWS = 8


def _cparams(sem):
    return pltpu.CompilerParams(dimension_semantics=sem, vmem_limit_bytes=VMEM_LIMIT_BYTES)


def _silu(x):
    return x * jax.nn.sigmoid(x)


def _log_sigmoid(x):
    return jnp.minimum(x, 0.0) - jnp.log(1.0 + jnp.exp(-jnp.abs(x)))


def _softplus(x):
    return jnp.maximum(x, 0.0) + jnp.log(1.0 + jnp.exp(-jnp.abs(x)))


def _nt(a, b):
    return lax.dot_general(a, b, (((1,), (1,)), ((), ())), preferred_element_type=F32)


def _tn(a, b):
    return lax.dot_general(a, b, (((0,), (0,)), ((), ())), preferred_element_type=F32)


def _dotx(a, b):
    return jnp.dot(a, b, precision=HIGHEST, preferred_element_type=F32)


def _stage_lhs(x_ref, g_ref, xs_ref):
    @pl.when(pl.program_id(1) == 0)
    def _():
        def body(r, carry):
            sl = pl.ds(pl.multiple_of(r * NORM_ROWS, NORM_ROWS), NORM_ROWS)
            x = x_ref[sl, :].astype(F32)
            if g_ref is not None:
                x = x * lax.rsqrt(jnp.mean(x * x, axis=-1, keepdims=True) + RMS_EPS) * g_ref[...]
            xs_ref[sl, :] = x.astype(BF16)
            return carry
        trips = x_ref.shape[0] // NORM_ROWS
        lax.fori_loop(0, trips, body, 0, unroll=max(u for u in (5, 4, 2, 1) if trips % u == 0))


def _mm_kernel(*refs, has_norm, has_resid, n_w, stage):
    it = iter(refs)
    x_ref = next(it)
    g_ref = next(it) if has_norm else None
    w_refs = [next(it) for _ in range(n_w)]
    r_ref = next(it) if has_resid else None
    o_ref = next(it)
    xs_ref = next(it) if stage else None

    if stage:
        _stage_lhs(x_ref, g_ref, xs_ref)
        xb = xs_ref[...]
    else:
        xb = x_ref[...]
    acc = [jnp.dot(xb, w[...].astype(BF16), preferred_element_type=F32) for w in w_refs]
    y = _silu(acc[0]) * acc[1] if n_w == 2 else acc[0]
    if has_resid:
        y = y + r_ref[...]
    o_ref[...] = y.astype(o_ref.dtype)


def _matmul(x, ws, *, tn, out_dtype, gain=None, resid=None, layer=None, name):
    m = x.shape[0]
    k, n = ws[0].shape[-2:]
    tm = ROW_TILE
    assert m % tm == 0 and n % tn == 0 and k % LANE == 0 and tm % NORM_ROWS == 0
    stage = gain is not None or x.dtype != BF16
    in_specs = [pl.BlockSpec((tm, k), lambda i, j: (i, 0), pipeline_mode=pl.Buffered(1))]
    args = [x]
    if gain is not None:
        in_specs.append(pl.BlockSpec((1, k), lambda i, j: (0, 0)))
        args.append(gain.reshape(1, k).astype(F32))
    for w in ws:
        if layer is None:
            in_specs.append(pl.BlockSpec((k, tn), lambda i, j: (0, j)))
        else:
            in_specs.append(pl.BlockSpec((None, k, tn), lambda i, j: (layer, 0, j)))
        args.append(w)
    if resid is not None:
        in_specs.append(pl.BlockSpec((tm, tn), lambda i, j: (i, j)))
        args.append(resid)
    scratch = [pltpu.VMEM((tm, k), BF16)] if stage else []
    return pl.pallas_call(
        functools.partial(_mm_kernel, has_norm=gain is not None, has_resid=resid is not None,
                          n_w=len(ws), stage=stage),
        out_shape=jax.ShapeDtypeStruct((m, n), out_dtype),
        grid=(m // tm, n // tn),
        in_specs=in_specs,
        out_specs=pl.BlockSpec((tm, tn), lambda i, j: (i, j)),
        scratch_shapes=scratch,
        compiler_params=_cparams(("parallel", "arbitrary")),
        name=name,
    )(*args)


def _ab_in_kernel(x_ref, g_ref, w_ref, wx_ref, ws_ref, o_ref, xs_ref, *, n_plain, n_main, shift):
    j = pl.program_id(1)
    tm, tn = o_ref.shape
    _stage_lhs(x_ref, g_ref, xs_ref)

    @pl.when(j < n_plain)
    def _():
        o_ref[...] = jnp.dot(xs_ref[...], w_ref[...].astype(BF16), preferred_element_type=F32)

    @pl.when((j >= n_plain) & (j < n_main))
    def _():
        w = jnp.concatenate([w_ref[...], wx_ref[...]], axis=1).astype(BF16)[:, shift:shift + tn]
        o_ref[...] = jnp.dot(xs_ref[...], w, preferred_element_type=F32)

    @pl.when(j == n_main)
    def _():
        y = jnp.dot(xs_ref[...], ws_ref[...].astype(BF16), preferred_element_type=F32)
        o_ref[...] = jnp.concatenate([y, jnp.zeros((tm, tn - y.shape[1]), F32)], axis=1)


def _ab_in_proj(x, gain, w_all, layer, w_small, *, aligned, shift, n_out, tn=512):
    m, k = x.shape
    tm = ROW_TILE
    assert m % tm == 0 and aligned % tn == 0 and n_out % tn == 0 and 0 < shift < LANE
    n_plain = aligned // tn
    n_main = n_out // tn
    sub = tn // LANE
    return pl.pallas_call(
        functools.partial(_ab_in_kernel, n_plain=n_plain, n_main=n_main, shift=shift),
        out_shape=jax.ShapeDtypeStruct((m, n_out + tn), F32),
        grid=(m // tm, n_main + 1),
        in_specs=[pl.BlockSpec((tm, k), lambda i, j: (i, 0), pipeline_mode=pl.Buffered(1)),
                  pl.BlockSpec((1, k), lambda i, j: (0, 0)),
                  pl.BlockSpec((None, k, tn), lambda i, j: (layer, 0, jnp.minimum(j, n_main - 1))),
                  pl.BlockSpec((None, k, LANE), lambda i, j: (layer, 0, sub * (jnp.clip(j, n_plain, n_main - 1) + 1))),
                  pl.BlockSpec((k, LANE), lambda i, j: (0, 0), pipeline_mode=pl.Buffered(1))],
        out_specs=pl.BlockSpec((tm, tn), lambda i, j: (i, j)),
        scratch_shapes=[pltpu.VMEM((tm, k), BF16)],
        compiler_params=_cparams(("parallel", "arbitrary")),
        name="ab_in_proj",
    )(x, gain.reshape(1, k).astype(F32), w_all, w_all, w_small)


def _rmsnorm_kernel(x_ref, g_ref, o_ref):
    x = x_ref[...]
    o_ref[...] = x * lax.rsqrt(jnp.mean(x * x, axis=-1, keepdims=True) + RMS_EPS) * g_ref[...]


def _rmsnorm_rows(x, gain, *, row0, n_rows, rows):
    d = x.shape[1]
    assert n_rows % rows == 0 and row0 % rows == 0
    return pl.pallas_call(
        _rmsnorm_kernel,
        out_shape=jax.ShapeDtypeStruct((n_rows, d), F32),
        grid=(n_rows // rows,),
        in_specs=[pl.BlockSpec((rows, d), lambda i: (row0 // rows + i, 0)), pl.BlockSpec((1, d), lambda i: (0, 0))],
        out_specs=pl.BlockSpec((rows, d), lambda i: (i, 0)),
        compiler_params=_cparams(("parallel",)),
        name="final_rmsnorm",
    )(x, gain.reshape(1, d))


def _gla_gate(sm, wg_ref, bg_ref):
    gp = jnp.dot(sm.astype(BF16), wg_ref[...].astype(BF16), preferred_element_type=F32) + bg_ref[...]
    return _log_sigmoid(gp) / GLA_GATE_TAU


def _gated_head_norm(o, nw_ref, gate):
    on = o * lax.rsqrt(jnp.mean(o * o, axis=-1, keepdims=True) + RMS_EPS) * nw_ref[...]
    return on * _silu(gate)


def _gla_prompt_kernel(q_ref, k_ref, v_ref, go_ref, sm_ref, wg_ref, bg_ref, nw_ref,
                       o_ref, st_ref, sT_ref):
    c = pl.program_id(2)
    dk = q_ref.shape[1]
    C = CHUNK
    sub = GLA_SUB

    @pl.when(c == 0)
    def _():
        sT_ref[...] = jnp.zeros_like(sT_ref)

    r_i = lax.broadcasted_iota(jnp.int32, (C, C), 0)
    c_i = lax.broadcasted_iota(jnp.int32, (C, C), 1)
    tril = (r_i >= c_i).astype(F32)
    key_row = lax.broadcasted_iota(jnp.int32, (C, 1), 0)
    lane = lax.broadcasted_iota(jnp.int32, (sub, C), 1)
    rloc = lax.broadcasted_iota(jnp.int32, (sub, C), 0)

    for cc in range(q_ref.shape[0] // C):
        rows = slice(cc * C, (cc + 1) * C)
        q = q_ref[rows, :] * (dk ** -0.5)
        k = k_ref[rows, :]
        v = v_ref[rows, :].astype(BF16)
        g = _gla_gate(sm_ref[rows, :], wg_ref, bg_ref)
        cum = _dotx(tril, g)
        blocks = []
        for i in range(C // sub):
            r0 = i * sub
            qi = q[r0:r0 + sub]
            cumi = cum[r0:r0 + sub]
            dg = jnp.zeros((sub, C), F32)
            for s in range(r0, r0 + sub):
                x = qi * k[s:s + 1] * jnp.exp(jnp.minimum(cumi - cum[s:s + 1], 0.0))
                dg = jnp.where(lane == s, jnp.sum(x, axis=1, keepdims=True), dg)
            att_i = jnp.where(rloc + r0 >= lane, dg, 0.0)
            if i > 0:
                ci = cum[r0 - 1:r0]
                qt = qi * jnp.exp(cumi - ci)
                kt = jnp.where(key_row < r0, k * jnp.exp(jnp.minimum(ci - cum, 0.0)), 0.0)
                att_i = att_i + _nt(qt.astype(BF16), kt.astype(BF16))
            blocks.append(att_i)
        att = jnp.concatenate(blocks, axis=0)

        sT = sT_ref[...]
        last = cum[C - 1:C]
        o = jnp.dot(att.astype(BF16), v, preferred_element_type=F32)
        o = o + _nt((q * jnp.exp(cum)).astype(BF16), sT.astype(BF16))
        kd = k * jnp.exp(last - cum)
        sT_ref[...] = sT * jnp.exp(last) + _tn(v, kd.astype(BF16))
        o_ref[rows, :] = _gated_head_norm(o, nw_ref, go_ref[rows, :]).astype(o_ref.dtype)

    @pl.when(c == pl.num_programs(2) - 1)
    def _():
        st_ref[0, 0] = sT_ref[...].T


def _gla_prompt(proj, small, cols, wg_pad, b_gate, gla_norm, *, batch, seq, dk, dv, mix_shape):
    step = GLA_STEP_CHUNKS * CHUNK
    nc = seq // step
    assert seq % step == 0
    h = GLA_HEADS
    qb, kb, vb, gb = cols["gq"] // dk, cols["gk"] // dk, cols["gv"] // dv, cols["g_out"] // dv
    row = lambda b, hh, c: b * nc + c
    in_specs = [
        pl.BlockSpec((step, dk), lambda b, hh, c: (row(b, hh, c), qb + hh)),
        pl.BlockSpec((step, dk), lambda b, hh, c: (row(b, hh, c), kb + hh)),
        pl.BlockSpec((step, dv), lambda b, hh, c: (row(b, hh, c), vb + hh)),
        pl.BlockSpec((step, dv), lambda b, hh, c: (row(b, hh, c), gb + hh)),
        pl.BlockSpec((step, LANE), lambda b, hh, c: (row(b, hh, c), cols["small_blk"])),
        pl.BlockSpec((LANE, dk), lambda b, hh, c: (0, hh)),
        pl.BlockSpec((1, dk), lambda b, hh, c: (0, hh)),
        pl.BlockSpec((1, dv), lambda b, hh, c: (0, 0)),
    ]
    return pl.pallas_call(
        _gla_prompt_kernel,
        out_shape=(jax.ShapeDtypeStruct(mix_shape, BF16),
                   jax.ShapeDtypeStruct((batch, h, dk, dv), F32)),
        grid=(batch, h, nc),
        in_specs=in_specs,
        out_specs=(pl.BlockSpec((step, dv), lambda b, hh, c: (row(b, hh, c), hh)),
                   pl.BlockSpec((1, 1, dk, dv), lambda b, hh, c: (b, hh, 0, 0))),
        scratch_shapes=[pltpu.VMEM((dv, dk), F32)],
        compiler_params=_cparams(("parallel", "parallel", "arbitrary")),
        name="gla_prompt",
    )(proj, proj, proj, proj, small, wg_pad, b_gate.reshape(1, -1), gla_norm.reshape(1, -1))


def _alias_outputs(in_specs, args, bufs):
    aliases = {}
    for k, buf in enumerate(bufs):
        if buf is not None:
            aliases[len(args)] = k
            in_specs.append(pl.BlockSpec(memory_space=pl.ANY))
            args.append(buf)
    return aliases


def _gla_sample_kernel(q_ref, k_ref, v_ref, go_ref, sm_ref, wg_ref, bg_ref, nw_ref, s_ref, *rest):
    o_ref, so_ref = rest[-2:]
    nb, dk = q_ref.shape
    qT = (q_ref[...] * (dk ** -0.5)).T
    kT = k_ref[...].T
    egT = jnp.exp(_gla_gate(sm_ref[...], wg_ref, bg_ref)).T
    v = v_ref[...]
    rows = []
    for b in range(nb):
        s_new = s_ref[b, 0] * egT[:, b:b + 1] + kT[:, b:b + 1] * v[b:b + 1]
        so_ref[b, 0] = s_new
        rows.append(jnp.sum(qT[:, b:b + 1] * s_new, axis=0, keepdims=True))
    o = jnp.concatenate(rows, axis=0)
    o_ref[...] = _gated_head_norm(o, nw_ref, go_ref[...]).astype(o_ref.dtype)


def _gla_sample(proj, small, cols, wg_pad, b_gate, gla_norm, states, j, prev, mix, *, row0, dk, dv):
    nb = SAMPLE_ROWS
    bs, h = states.shape[1:3]
    assert bs % nb == 0 and row0 % nb == 0
    r0 = row0 // nb
    qb, kb, vb, gb = cols["gq"] // dk, cols["gk"] // dk, cols["gv"] // dv, cols["g_out"] // dv
    in_specs = [
        pl.BlockSpec((nb, dk), lambda i, hh: (r0 + i, qb + hh)),
        pl.BlockSpec((nb, dk), lambda i, hh: (r0 + i, kb + hh)),
        pl.BlockSpec((nb, dv), lambda i, hh: (r0 + i, vb + hh)),
        pl.BlockSpec((nb, dv), lambda i, hh: (r0 + i, gb + hh)),
        pl.BlockSpec((nb, LANE), lambda i, hh: (r0 + i, cols["small_blk"])),
        pl.BlockSpec((LANE, dk), lambda i, hh: (0, hh)),
        pl.BlockSpec((1, dk), lambda i, hh: (0, hh)),
        pl.BlockSpec((1, dv), lambda i, hh: (0, 0)),
        pl.BlockSpec((None, nb, 1, dk, dv), lambda i, hh: (j, i, hh, 0, 0)),
    ]
    args = [proj, proj, proj, proj, small, wg_pad, b_gate.reshape(1, -1), gla_norm.reshape(1, -1), states]
    aliases = _alias_outputs(in_specs, args, [mix, prev])
    return pl.pallas_call(
        _gla_sample_kernel,
        out_shape=(jax.ShapeDtypeStruct(mix.shape, BF16),
                   jax.ShapeDtypeStruct(states.shape, F32)),
        grid=(bs // nb, h),
        in_specs=in_specs,
        out_specs=(pl.BlockSpec((nb, dv), lambda i, hh: (r0 + i, hh)),
                   pl.BlockSpec((None, nb, 1, dk, dv), lambda i, hh: (j, i, hh, 0, 0))),
        input_output_aliases=aliases,
        compiler_params=_cparams(("parallel", "parallel")),
        name="gla_sample",
    )(*args)


def _dn_gates(sm, alog_ref, dtb_ref):
    decay = -jnp.exp(alog_ref[...]) * _softplus(sm + dtb_ref[...])
    return decay, jax.nn.sigmoid(sm)


def _lane_col(x, lane_idx):
    lane = lax.broadcasted_iota(jnp.int32, x.shape, 1)
    return jnp.sum(jnp.where(lane == lane_idx, x, 0.0), axis=1, keepdims=True)


def _l2norm(x):
    return x * lax.rsqrt(jnp.sum(x * x, axis=-1, keepdims=True) + L2_EPS)


def _split_bf16(x):
    hi = x.astype(BF16)
    return hi, (x - hi.astype(F32)).astype(BF16)


def _bmm3(a, b):
    ah, al = _split_bf16(a)
    bh, bl = _split_bf16(b)
    mm = lambda x, y: jnp.einsum('gij,gjk->gik', x, y, preferred_element_type=F32)
    return mm(ah, bh) + (mm(ah, bl) + mm(al, bh))


def _unit_lower_inverse(a, sub):
    C = a.shape[-1]
    assert C // sub == 4
    r = lax.broadcasted_iota(jnp.int32, (C, C), 0)
    c = lax.broadcasted_iota(jnp.int32, (C, C), 1)
    eye = (r == c).astype(F32)
    same_blk = (r // sub) == (c // sub)
    x = jnp.where(same_blk, -a, 0.0)
    low = jnp.where(same_blk, 0.0, a)
    td = eye + x
    p = x
    for _ in range(int(math.log2(sub)) - 1):
        p = _bmm3(p, p)
        td = _bmm3(td, eye + p)
    m = _bmm3(td, low)
    tm = _bmm3(eye - m, eye + _bmm3(m, m))
    return _bmm3(tm, td)


def _heads(x, n):
    return jnp.stack([x[:, h * DN_HEAD:(h + 1) * DN_HEAD] for h in range(n)], axis=0)


def _unheads(x):
    return jnp.concatenate([x[h] for h in range(x.shape[0])], axis=1)


def _dn_prep_kernel(q_ref, k_ref, v_ref, sm_ref, wq_ref, wk_ref, wv_ref, alog_ref, dtb_ref,
                    u_ref, w_ref, qe_ref, kd_ref, att_ref, cum_ref, cq_ref, ck_ref, cv_ref, prev_ref,
                    *, a_lane, b_lane):
    hg = pl.program_id(1)
    c = pl.program_id(2)
    C = q_ref.shape[0]
    d = DN_HEAD
    G = q_ref.shape[1] // d

    @pl.when(c == 0)
    def _():
        prev_ref[...] = jnp.zeros_like(prev_ref)

    def conv(x_ref, w_ref_, slot):
        x = x_ref[...]
        ext = jnp.concatenate([prev_ref[slot], x], axis=0)
        y = x * w_ref_[CONV_WIDTH - 1:CONV_WIDTH]
        for i in range(1, CONV_WIDTH):
            y = y + pltpu.roll(ext, i, 0)[8:8 + C] * w_ref_[CONV_WIDTH - 1 - i:CONV_WIDTH - i]
        prev_ref[slot] = x[C - 8:C]
        return _silu(y)

    q3 = _heads(conv(q_ref, wq_ref, 0), G)
    k3 = _heads(conv(k_ref, wk_ref, 1), G)
    v2 = _heads(conv(v_ref, wv_ref, 2), G)
    q2 = _l2norm(q3) * (d ** -0.5)
    k2 = _l2norm(k3)

    decay, beta_all = _dn_gates(sm_ref[...], alog_ref, dtb_ref)
    r_i = lax.broadcasted_iota(jnp.int32, (C, C), 0)
    c_i = lax.broadcasted_iota(jnp.int32, (C, C), 1)
    cum_all = _dotx((r_i >= c_i).astype(F32), decay)
    cum_ref[...] = cum_all
    r_w = lax.broadcasted_iota(jnp.int32, (C, LANE), 0)
    c_w = lax.broadcasted_iota(jnp.int32, (C, LANE), 1)
    cums, betas, cum_rows = [], [], []
    for hl in range(G):
        head = hg * G + hl
        cum_h = _lane_col(cum_all, a_lane + head)
        cums.append(cum_h)
        betas.append(_lane_col(beta_all, b_lane + head))
        cum_rows.append(jnp.sum(jnp.where(r_w == c_w, cum_h, 0.0), axis=0, keepdims=True))
    cum = jnp.stack(cums, axis=0)
    beta = jnp.stack(betas, axis=0)
    cum_row = jnp.stack(cum_rows, axis=0)
    dec_w = jnp.exp(jnp.minimum(cum - cum_row, 0.0))
    dec = dec_w[:, :, :C]

    kbeta = k2 * beta
    k2b = k2.astype(BF16)
    kk = jnp.einsum('gtd,gsd->gts', kbeta.astype(BF16), k2b, preferred_element_type=F32)
    t = _unit_lower_inverse(jnp.where(r_i > c_i, kk * dec, 0.0), DN_SUB)
    ecum = jnp.exp(cum)
    sol = _bmm3(t, jnp.concatenate([v2 * beta, kbeta * ecum], axis=-1))
    k_wide = jnp.concatenate([k2b, jnp.zeros_like(k2b)], axis=1)
    qk = jnp.einsum('gtd,gsd->gts', q2.astype(BF16), k_wide, preferred_element_type=F32)
    att = jnp.where(r_w >= c_w, qk * dec_w, 0.0)
    last = cum[:, C - 1:C]

    u_ref[...] = _unheads(sol[:, :, :d])
    w_ref[...] = _unheads(sol[:, :, d:]).astype(BF16)
    qe_ref[...] = _unheads(q2 * ecum).astype(BF16)
    kd_ref[...] = _unheads(k2 * jnp.exp(last - cum)).astype(BF16)
    att_ref[...] = _unheads(att).astype(BF16)

    @pl.when(c == pl.num_programs(2) - 1)
    def _():
        cq_ref[0] = prev_ref[0]
        ck_ref[0] = prev_ref[1]
        cv_ref[0] = prev_ref[2]


def _dn_scan_kernel(u_ref, w_ref, qe_ref, kd_ref, att_ref, cum_ref, z_ref, nw_ref, *rest, a_lane, group):
    o_ref, st_ref, s_ref = rest[-3:]
    c = pl.program_id(1)
    C = u_ref.shape[0]
    H = DN_HEADS

    @pl.when(c == 0)
    def _():
        s_ref[...] = jnp.zeros_like(s_ref)

    s = s_ref[...]
    sb = s.astype(BF16)
    v_new = _heads(u_ref[...], H) - jnp.einsum('hcd,hdv->hcv', _heads(w_ref[...], H), sb,
                                               preferred_element_type=F32)
    vb = v_new.astype(BF16)
    o = jnp.einsum('hts,hsv->htv', _heads(att_ref[...], H)[:, :, :C], vb, preferred_element_type=F32)
    o = o + jnp.einsum('hcd,hdv->hcv', _heads(qe_ref[...], H), sb, preferred_element_type=F32)
    cum_last = cum_ref[C - 1:C, :]
    lanes = [(h // group) * LANE + a_lane + h for h in range(H)]
    e_last = jnp.exp(jnp.stack([cum_last[:, l:l + 1] for l in lanes], axis=0))
    s_new = s * e_last + jnp.einsum('hcd,hcv->hdv', _heads(kd_ref[...], H), vb, preferred_element_type=F32)
    s_ref[...] = s_new
    on = o * lax.rsqrt(jnp.mean(o * o, axis=-1, keepdims=True) + RMS_EPS) * nw_ref[...]
    o_ref[...] = _unheads(on * _silu(_heads(z_ref[...], H))).astype(o_ref.dtype)

    @pl.when(c == pl.num_programs(1) - 1)
    def _():
        st_ref[0] = s_new


def _dn_prompt(proj, small, cols, conv_w, j, alog_pad, dtb_pad, dn_norm, mix, *, batch, seq):
    nc = seq // CHUNK
    assert seq % CHUNK == 0
    G, d = DN_GROUP, DN_HEAD
    gw = G * d
    ng = DN_HEADS // G
    dn_k = DN_HEADS * d
    t = batch * seq
    qb, kb, vb = cols["dn_q"] // gw, cols["dn_k"] // gw, cols["dn_v"] // gw
    row = lambda b, g, c: b * nc + c
    wide = lambda dt: jax.ShapeDtypeStruct((t, dn_k), dt)
    tok = pl.BlockSpec((CHUNK, gw), lambda b, g, c: (row(b, g, c), g))
    tail = pl.BlockSpec((1, 8, gw), lambda b, g, c: (b, 0, g))
    u, w, qe, kd, att, cum, cq, ck, cv = pl.pallas_call(
        functools.partial(_dn_prep_kernel, a_lane=cols["a_lane"], b_lane=cols["b_lane"]),
        out_shape=(wide(F32), wide(BF16), wide(BF16), wide(BF16), wide(BF16),
                   jax.ShapeDtypeStruct((t, ng * LANE), F32),
                   jax.ShapeDtypeStruct((batch, 8, dn_k), F32), jax.ShapeDtypeStruct((batch, 8, dn_k), F32),
                   jax.ShapeDtypeStruct((batch, 8, dn_k), F32)),
        grid=(batch, ng, nc),
        in_specs=[
            pl.BlockSpec((CHUNK, gw), lambda b, g, c: (row(b, g, c), qb + g)),
            pl.BlockSpec((CHUNK, gw), lambda b, g, c: (row(b, g, c), kb + g)),
            pl.BlockSpec((CHUNK, gw), lambda b, g, c: (row(b, g, c), vb + g)),
            pl.BlockSpec((CHUNK, LANE), lambda b, g, c: (row(b, g, c), cols["small_blk"])),
            pl.BlockSpec((None, CONV_WIDTH, gw), lambda b, g, c: (j, 0, g)),
            pl.BlockSpec((None, CONV_WIDTH, gw), lambda b, g, c: (j, 0, dn_k // gw + g)),
            pl.BlockSpec((None, CONV_WIDTH, gw), lambda b, g, c: (j, 0, 2 * dn_k // gw + g)),
            pl.BlockSpec((1, LANE), lambda b, g, c: (0, 0)),
            pl.BlockSpec((1, LANE), lambda b, g, c: (0, 0)),
        ],
        out_specs=(tok, tok, tok, tok, tok,
                   pl.BlockSpec((CHUNK, LANE), lambda b, g, c: (row(b, g, c), g)), tail, tail, tail),
        scratch_shapes=[pltpu.VMEM((3, 8, gw), F32)],
        compiler_params=_cparams(("parallel", "parallel", "arbitrary")),
        name="deltanet_prep",
    )(proj, proj, proj, small, conv_w, conv_w, conv_w, alog_pad, dtb_pad)

    full = lambda width: pl.BlockSpec((CHUNK, width), lambda b, c: (b * nc + c, 0))
    in_specs = [full(dn_k), full(dn_k), full(dn_k), full(dn_k), full(dn_k), full(ng * LANE),
                pl.BlockSpec((CHUNK, dn_k), lambda b, c: (b * nc + c, cols["dn_z"] // dn_k)),
                pl.BlockSpec((1, d), lambda b, c: (0, 0))]
    args = [u, w, qe, kd, att, cum, proj, dn_norm.reshape(1, -1)]
    aliases = _alias_outputs(in_specs, args, [mix])
    mix, st = pl.pallas_call(
        functools.partial(_dn_scan_kernel, a_lane=cols["a_lane"], group=G),
        out_shape=(jax.ShapeDtypeStruct(mix.shape, BF16),
                   jax.ShapeDtypeStruct((batch, DN_HEADS, d, d), F32)),
        grid=(batch, nc),
        in_specs=in_specs,
        out_specs=(pl.BlockSpec((CHUNK, dn_k), lambda b, c: (b * nc + c, mix.shape[1] // dn_k - 1)),
                   pl.BlockSpec((1, DN_HEADS, d, d), lambda b, c: (b, 0, 0, 0))),
        scratch_shapes=[pltpu.VMEM((DN_HEADS, d, d), F32)],
        input_output_aliases=aliases,
        compiler_params=_cparams(("parallel", "arbitrary")),
        name="deltanet_scan",
    )(*args)
    tail_rows = slice(8 - (CONV_WIDTH - 1), 8)
    conv_new = jnp.concatenate([cq[:, tail_rows], ck[:, tail_rows], cv[:, tail_rows]], axis=-1)
    return mix, st, conv_new


def _dn_sample_kernel(q_ref, k_ref, v_ref, z_ref, sm_ref, bq_ref, bk_ref, bv_ref, wq_ref, wk_ref, wv_ref,
                      alog_ref, dtb_ref, nw_ref, s_ref, *rest, a_lane, b_lane):
    o_ref, so_ref = rest[-2:]
    hg = pl.program_id(1)
    nb = q_ref.shape[0]
    G = s_ref.shape[1]
    d = DN_HEAD

    def conv(x_ref, buf_ref, w_ref):
        y = x_ref[...] * w_ref[CONV_WIDTH - 1:CONV_WIDTH]
        for i in range(CONV_WIDTH - 1):
            y = y + buf_ref[:, i, :] * w_ref[i:i + 1]
        return _silu(y)

    yq = conv(q_ref, bq_ref, wq_ref)
    yk = conv(k_ref, bk_ref, wk_ref)
    yv = conv(v_ref, bv_ref, wv_ref)
    decay, beta_all = _dn_gates(sm_ref[...], alog_ref, dtb_ref)
    z = z_ref[...]
    outs = []
    for hl in range(G):
        head = hg * G + hl
        sl = slice(hl * d, (hl + 1) * d)
        q2 = _l2norm(yq[:, sl]) * (d ** -0.5)
        k2 = _l2norm(yk[:, sl])
        v2 = yv[:, sl]
        eg = jnp.exp(_lane_col(decay, a_lane + head))
        beta = _lane_col(beta_all, b_lane + head)
        qT = q2.T
        kT = k2.T
        rows = []
        for b in range(nb):
            s = s_ref[b, hl]
            kc = kT[:, b:b + 1]
            ks = jnp.sum(kc * s, axis=0, keepdims=True)
            v_new = beta[b:b + 1] * (v2[b:b + 1] - eg[b:b + 1] * ks)
            s_new = s * eg[b:b + 1] + kc * v_new
            so_ref[b, hl] = s_new
            rows.append(jnp.sum(qT[:, b:b + 1] * s_new, axis=0, keepdims=True))
        outs.append(_gated_head_norm(jnp.concatenate(rows, axis=0), nw_ref, z[:, sl]))
    o_ref[...] = jnp.concatenate(outs, axis=1).astype(o_ref.dtype)


def _dn_sample(proj, small, cols, conv_w, j, alog_pad, dtb_pad, dn_norm, states, conv_bufs, prev, mix, *, row0):
    nb = SAMPLE_ROWS
    bs = states.shape[1]
    G, d = DN_GROUP, DN_HEAD
    gw = G * d
    ng = DN_HEADS // G
    dn_k = DN_HEADS * d
    assert bs % nb == 0 and row0 % nb == 0
    r0 = row0 // nb
    qb, kb, vb, zb = cols["dn_q"] // gw, cols["dn_k"] // gw, cols["dn_v"] // gw, cols["dn_z"] // gw
    ob = (mix.shape[1] - dn_k) // gw
    nbuf = CONV_WIDTH - 1
    in_specs = [
        pl.BlockSpec((nb, gw), lambda i, g: (r0 + i, qb + g)),
        pl.BlockSpec((nb, gw), lambda i, g: (r0 + i, kb + g)),
        pl.BlockSpec((nb, gw), lambda i, g: (r0 + i, vb + g)),
        pl.BlockSpec((nb, gw), lambda i, g: (r0 + i, zb + g)),
        pl.BlockSpec((nb, LANE), lambda i, g: (r0 + i, cols["small_blk"])),
        pl.BlockSpec((None, nb, nbuf, gw), lambda i, g: (j, i, 0, g)),
        pl.BlockSpec((None, nb, nbuf, gw), lambda i, g: (j, i, 0, dn_k // gw + g)),
        pl.BlockSpec((None, nb, nbuf, gw), lambda i, g: (j, i, 0, 2 * dn_k // gw + g)),
        pl.BlockSpec((None, CONV_WIDTH, gw), lambda i, g: (j, 0, g)),
        pl.BlockSpec((None, CONV_WIDTH, gw), lambda i, g: (j, 0, dn_k // gw + g)),
        pl.BlockSpec((None, CONV_WIDTH, gw), lambda i, g: (j, 0, 2 * dn_k // gw + g)),
        pl.BlockSpec((1, LANE), lambda i, g: (0, 0)),
        pl.BlockSpec((1, LANE), lambda i, g: (0, 0)),
        pl.BlockSpec((1, d), lambda i, g: (0, 0)),
        pl.BlockSpec((None, nb, G, d, d), lambda i, g: (j, i, g, 0, 0)),
    ]
    args = [proj, proj, proj, proj, small, conv_bufs, conv_bufs, conv_bufs, conv_w, conv_w, conv_w,
            alog_pad, dtb_pad, dn_norm.reshape(1, -1), states]
    aliases = _alias_outputs(in_specs, args, [mix, prev])
    return pl.pallas_call(
        functools.partial(_dn_sample_kernel, a_lane=cols["a_lane"], b_lane=cols["b_lane"]),
        out_shape=(jax.ShapeDtypeStruct(mix.shape, BF16),
                   jax.ShapeDtypeStruct(states.shape, F32)),
        grid=(bs // nb, ng),
        in_specs=in_specs,
        out_specs=(pl.BlockSpec((nb, gw), lambda i, g: (r0 + i, ob + g)),
                   pl.BlockSpec((None, nb, G, d, d), lambda i, g: (j, i, g, 0, 0))),
        input_output_aliases=aliases,
        compiler_params=_cparams(("parallel", "parallel")),
        name="deltanet_sample",
    )(*args)


def _mla_post_kernel(ckv_ref, kpe_ref, cs_ref, g_ref, lat_ref, kr_ref):
    x = ckv_ref[...]
    lat_ref[...] = x * lax.rsqrt(jnp.mean(x * x, axis=-1, keepdims=True) + RMS_EPS) * g_ref[...]
    zz = kpe_ref[...] * cs_ref[...]
    kr_ref[...] = zz + pltpu.roll(zz, QK_ROPE, 1)


def _mla_post(a, cs, kv_norm, *, rows=520):
    m = a.shape[0]
    assert m % rows == 0
    ckv_blk = Q_LORA // KV_LORA
    kpe_blk = (Q_LORA + KV_LORA) // LANE
    return pl.pallas_call(
        _mla_post_kernel,
        out_shape=(jax.ShapeDtypeStruct((m, KV_LORA), F32), jax.ShapeDtypeStruct((m, LANE), F32)),
        grid=(m // rows,),
        in_specs=[pl.BlockSpec((rows, KV_LORA), lambda i: (i, ckv_blk)),
                  pl.BlockSpec((rows, LANE), lambda i: (i, kpe_blk)),
                  pl.BlockSpec((rows, LANE), lambda i: (i, 0)),
                  pl.BlockSpec((1, KV_LORA), lambda i: (0, 0))],
        out_specs=(pl.BlockSpec((rows, KV_LORA), lambda i: (i, 0)),
                   pl.BlockSpec((rows, LANE), lambda i: (i, 0))),
        compiler_params=_cparams(("parallel",)),
        name="mla_latent_rope",
    )(a, a, cs, kv_norm.reshape(1, -1))


def _mla_prefill_kernel(qn_ref, qr_ref, cs_ref, kn_ref, v_ref, kr_ref, o_ref, *, scale, tile):
    n = qn_ref.shape[0] // tile
    kf = jnp.concatenate([kn_ref[...], kr_ref[...].astype(BF16)], axis=1)
    qloc = lax.broadcasted_iota(jnp.int32, (tile, tile), 0)
    kloc = lax.broadcasted_iota(jnp.int32, (tile, tile), 1)
    for qi in range(n):
        rows = slice(qi * tile, (qi + 1) * tile)
        q = jnp.concatenate([(qn_ref[rows, :] * scale).astype(BF16),
                             (qr_ref[rows, :] * cs_ref[rows, :] * scale).astype(BF16)], axis=1)
        for kj in range(qi + 1):
            ks = slice(kj * tile, (kj + 1) * tile)
            s = _nt(q, kf[ks])
            if kj == qi:
                s = jnp.where(kloc <= qloc, s, -jnp.inf)
            s_max = jnp.max(s, axis=-1, keepdims=True)
            if kj == 0:
                m = s_max
                p = jnp.exp(s - m)
                l = jnp.sum(p, axis=-1, keepdims=True)
                acc = jnp.dot(p.astype(BF16), v_ref[ks, :], preferred_element_type=F32)
            else:
                m_new = jnp.maximum(m, s_max)
                corr = jnp.exp(m - m_new)
                p = jnp.exp(s - m_new)
                l = l * corr + jnp.sum(p, axis=-1, keepdims=True)
                acc = acc * corr + jnp.dot(p.astype(BF16), v_ref[ks, :], preferred_element_type=F32)
                m = m_new
        o_ref[rows, :] = (acc / l).astype(o_ref.dtype)


def _mla_prefill(q, cs, kv, kr2, *, batch, seq, out_rows):
    assert seq % ATT_TQ == 0
    h = MLA_HEADS
    scale = (QK_NOPE + QK_ROPE) ** -0.5
    return pl.pallas_call(
        functools.partial(_mla_prefill_kernel, scale=scale, tile=ATT_TQ),
        out_shape=jax.ShapeDtypeStruct((out_rows, h * V_HEAD), BF16),
        grid=(batch, h),
        in_specs=[
            pl.BlockSpec((seq, QK_NOPE), lambda b, hh: (b, hh)),
            pl.BlockSpec((seq, LANE), lambda b, hh: (b, h + hh)),
            pl.BlockSpec((seq, LANE), lambda b, hh: (b, 0)),
            pl.BlockSpec((seq, QK_NOPE), lambda b, hh: (b, 2 * hh)),
            pl.BlockSpec((seq, V_HEAD), lambda b, hh: (b, 2 * hh + 1)),
            pl.BlockSpec((seq, LANE), lambda b, hh: (b, 0)),
        ],
        out_specs=pl.BlockSpec((seq, V_HEAD), lambda b, hh: (b, hh)),
        compiler_params=_cparams(("parallel", "parallel")),
        name="mla_prefill",
    )(q, q, cs, kv, kv, kr2)


def _absorb_q_kernel(x_ref, w_ref, o_ref):
    o_ref[...] = _nt(x_ref[...].astype(BF16), w_ref[...].astype(BF16)).astype(o_ref.dtype)


def _absorb_q(q, w_kvb, j, *, row0, rows):
    h = MLA_HEADS
    assert row0 % rows == 0
    return pl.pallas_call(
        _absorb_q_kernel,
        out_shape=jax.ShapeDtypeStruct((rows, h * KV_LORA), BF16),
        grid=(h,),
        in_specs=[pl.BlockSpec((rows, QK_NOPE), lambda hh: (row0 // rows, hh)),
                  pl.BlockSpec((None, KV_LORA, QK_NOPE), lambda hh: (j, 0, 2 * hh))],
        out_specs=pl.BlockSpec((rows, KV_LORA), lambda hh: (0, hh)),
        compiler_params=_cparams(("parallel",)),
        name="mla_absorb_q",
    )(q, w_kvb)


def _absorb_v_kernel(x_ref, w_ref, *rest):
    o_ref = rest[-1]
    o_ref[...] = jnp.dot(x_ref[...].astype(BF16), w_ref[...].astype(BF16),
                         preferred_element_type=F32).astype(o_ref.dtype)


def _absorb_v(o_lat, w_kvb, j, att, *, row0):
    rows = o_lat.shape[0]
    h = MLA_HEADS
    assert row0 % rows == 0
    in_specs = [pl.BlockSpec((rows, KV_LORA), lambda hh: (0, hh)),
                pl.BlockSpec((None, KV_LORA, V_HEAD), lambda hh: (j, 0, 2 * hh + 1))]
    args = [o_lat, w_kvb]
    aliases = _alias_outputs(in_specs, args, [att])
    return pl.pallas_call(
        _absorb_v_kernel,
        out_shape=jax.ShapeDtypeStruct(att.shape, BF16),
        grid=(h,),
        in_specs=in_specs,
        out_specs=pl.BlockSpec((rows, V_HEAD), lambda hh: (row0 // rows, hh)),
        input_output_aliases=aliases,
        compiler_params=_cparams(("parallel",)),
        name="mla_absorb_v",
    )(*args)


def _mla_decode_kernel(pt_ref, ql_ref, qr_ref, cs_ref, cn_ref, rn_ref, lat_hbm, rope_hbm, o_ref,
                       lat_buf, rope_buf, sem, m_ref, l_ref, acc_ref, *, scale, n_pages, layer):
    g = pl.program_id(1)
    n_groups = pl.num_programs(1)
    t = pl.program_id(0) * n_groups + g
    slot = t % 2

    def page_copies(step, sl):
        out = []
        for i in range(n_pages):
            pid = pt_ref[step * n_pages + i]
            out.append(pltpu.make_async_copy(lat_hbm.at[layer, pid], lat_buf.at[sl, i], sem.at[sl, 0]))
            out.append(pltpu.make_async_copy(rope_hbm.at[layer, pid], rope_buf.at[sl, i], sem.at[sl, 1]))
        return out

    @pl.when(t == 0)
    def _():
        for cp in page_copies(t, slot):
            cp.start()

    @pl.when(t + 1 < pl.num_programs(0) * n_groups)
    def _():
        for cp in page_copies(t + 1, 1 - slot):
            cp.start()

    for cp in page_copies(t, slot):
        cp.wait()
    lat_refs = [lat_buf.at[slot, i] for i in range(n_pages)]
    rope_refs = [rope_buf.at[slot, i] for i in range(n_pages)]
    ql = ql_ref[0]
    zz = qr_ref[0] * cs_ref[...]

    @pl.when(g == 0)
    def _():
        s0 = (jnp.sum(ql.astype(F32) * cn_ref[0], axis=-1, keepdims=True)
              + jnp.sum(zz * rn_ref[0], axis=-1, keepdims=True)) * scale
        m_ref[...] = s0
        l_ref[...] = jnp.ones_like(l_ref)
        acc_ref[...] = jnp.broadcast_to(cn_ref[0], acc_ref.shape)

    qrot = (zz + pltpu.roll(zz, QK_ROPE, 1))[:, :QK_ROPE].astype(BF16)
    cbs = [lat_refs[i][...].astype(BF16) for i in range(n_pages)]
    s = jnp.concatenate(
        [_nt(ql, cbs[i]) + jnp.dot(qrot, rope_refs[i][...].astype(BF16), preferred_element_type=F32)
         for i in range(n_pages)], axis=1) * scale
    m = m_ref[...]
    m_new = jnp.maximum(m, jnp.max(s, axis=-1, keepdims=True))
    corr = jnp.exp(m - m_new)
    e = jnp.exp(s - m_new)
    l_ref[...] = l_ref[...] * corr + jnp.sum(e, axis=-1, keepdims=True)
    eb = e.astype(BF16)
    page = cbs[0].shape[0]
    pv = jnp.dot(eb[:, :page], cbs[0], preferred_element_type=F32)
    for i in range(1, n_pages):
        pv = pv + jnp.dot(eb[:, i * page:(i + 1) * page], cbs[i], preferred_element_type=F32)
    acc_ref[...] = acc_ref[...] * corr + pv
    m_ref[...] = m_new

    @pl.when(g == pl.num_programs(1) - 1)
    def _():
        o_ref[0] = acc_ref[...] / l_ref[...]


def _mla_decode(q_lat, q_rope, cs_row, lat_new, kr_new, cache_latent, cache_k_rope_t, page_table, layer):
    bs = q_lat.shape[0]
    h = MLA_HEADS
    n_log = page_table.shape[1]
    page = cache_latent.shape[2]
    npg = DEC_PAGES
    assert n_log % npg == 0
    scale = (QK_NOPE + QK_ROPE) ** -0.5

    in_specs = [
        pl.BlockSpec((1, h, KV_LORA), lambda b, g, pt: (b, 0, 0)),
        pl.BlockSpec((1, h, LANE), lambda b, g, pt: (b, 0, 0)),
        pl.BlockSpec((1, LANE), lambda b, g, pt: (0, 0)),
        pl.BlockSpec((1, 1, KV_LORA), lambda b, g, pt: (b, 0, 0)),
        pl.BlockSpec((1, 1, LANE), lambda b, g, pt: (b, 0, 0)),
        pl.BlockSpec(memory_space=pl.ANY),
        pl.BlockSpec(memory_space=pl.ANY),
    ]
    grid_spec = pltpu.PrefetchScalarGridSpec(
        num_scalar_prefetch=1,
        grid=(bs, n_log // npg),
        in_specs=in_specs,
        out_specs=pl.BlockSpec((1, h, KV_LORA), lambda b, g, pt: (b, 0, 0)),
        scratch_shapes=[pltpu.VMEM((2, npg, page, KV_LORA), F32), pltpu.VMEM((2, npg, QK_ROPE, page), F32),
                        pltpu.SemaphoreType.DMA((2, 2)),
                        pltpu.VMEM((h, 1), F32), pltpu.VMEM((h, 1), F32), pltpu.VMEM((h, KV_LORA), F32)],
    )
    return pl.pallas_call(
        functools.partial(_mla_decode_kernel, scale=scale, n_pages=npg, layer=layer),
        out_shape=jax.ShapeDtypeStruct((bs, h, KV_LORA), F32),
        grid_spec=grid_spec,
        compiler_params=_cparams(("arbitrary", "arbitrary")),
        name="mla_decode",
    )(page_table.reshape(-1), q_lat.reshape(bs, h, KV_LORA), q_rope.reshape(bs, h, LANE), cs_row,
      lat_new.reshape(bs, 1, KV_LORA), kr_new.reshape(bs, 1, LANE), cache_latent, cache_k_rope_t)


def _rot_cols(w):
    half = QK_ROPE // 2
    return jnp.concatenate([-w[..., half:], w[..., :half]], axis=-1)


def _ab_columns(d_model):
    gla_k = GLA_HEADS * (d_model // 16)
    gla_v = GLA_HEADS * (d_model // 8)
    dn = DN_HEADS * DN_HEAD
    aligned = 2 * gla_k + gla_v
    n_main = aligned + gla_v + 4 * dn
    cols = dict(gq=0, gk=gla_k, gv=2 * gla_k, g_out=aligned, dn_q=aligned + gla_v, dn_k=aligned + gla_v + dn,
                dn_v=aligned + gla_v + 2 * dn, dn_z=aligned + gla_v + 3 * dn, small_blk=n_main // LANE,
                a_lane=GLA_GATE_RANK, b_lane=GLA_GATE_RANK + DN_HEADS)
    return cols, aligned, n_main


def _ab_small_weight(w_all, j, aligned, n_main):
    lo = w_all[j, :, aligned:aligned + GLA_GATE_RANK]
    hi = w_all[j, :, n_main + GLA_GATE_RANK:]
    pad = jnp.zeros((lo.shape[0], LANE - GLA_GATE_RANK - hi.shape[1]), lo.dtype)
    return jnp.concatenate([lo, hi, pad], axis=1)


def _pad_lanes(v, lane0):
    return jnp.zeros((1, LANE), F32).at[0, lane0:lane0 + v.shape[0]].set(v.astype(F32))


def _rope_table(pos):
    half = QK_ROPE // 2
    inv = ROPE_THETA ** (-jnp.arange(half, dtype=F32) / half)
    ang = pos.astype(F32)[:, None] * inv[None, :]
    cos, sin = jnp.cos(ang), jnp.sin(ang)
    return jnp.concatenate([cos, cos, sin, sin], axis=1)


def _mla_mixer(x, cs, norm_w, w_a, q_norm, kv_norm, w_qb, w_kvb, w_o, cache_latent, cache_k_rope_t,
               page_table, j, *, bp, lp, bs):
    tp = bp * lp
    d_model = x.shape[1]
    kpe_w = w_a[:, Q_LORA + KV_LORA:]
    a_pad = jnp.zeros((d_model, -(Q_LORA + KV_LORA + LANE) % 512), F32)
    w_a2 = jnp.concatenate([w_a[:, :Q_LORA + KV_LORA], kpe_w, _rot_cols(kpe_w), a_pad], axis=1)
    a = _matmul(x, [w_a2], tn=512, out_dtype=F32, gain=norm_w, name="mla_a_proj")
    latent, kr2 = _mla_post(a, cs, kv_norm)
    w_qb = w_qb.reshape(Q_LORA, MLA_HEADS, QK_NOPE + QK_ROPE)
    w_rope = w_qb[..., QK_NOPE:]
    w_q2 = jnp.concatenate(
        [w_qb[..., :QK_NOPE].reshape(Q_LORA, -1),
         jnp.concatenate([w_rope, _rot_cols(w_rope)], axis=-1).reshape(Q_LORA, -1)], axis=1)
    q = _matmul(a, [w_q2], tn=2048, out_dtype=F32, gain=q_norm, name="mla_q_proj")
    kv = _matmul(latent, [w_kvb], tn=2048, out_dtype=BF16, layer=j, name="mla_kv_proj")
    att = _mla_prefill(q, cs, kv, kr2, batch=bp, seq=lp, out_rows=x.shape[0])
    q_lat = _absorb_q(q, w_kvb, j, row0=tp, rows=bs)
    q_rope_s = q[tp:, MLA_HEADS * QK_NOPE:]
    o_lat = _mla_decode(q_lat, q_rope_s, cs[tp:tp + 1], latent[tp:], kr2[tp:],
                        cache_latent, cache_k_rope_t, page_table, j)
    att = _absorb_v(o_lat.reshape(bs, MLA_HEADS * KV_LORA), w_kvb, j, att, row0=tp)
    x = _matmul(att, [w_o], tn=512, out_dtype=F32, resid=x, layer=j, name="mla_out_proj")
    return x, latent, kr2


def kernel(x_prompt, x_sample, state_gla, state_delta, state_conv, cache_latent, cache_k_rope, page_table,
           norm_mix, norm_ffn, norm_final, ab_w_in, gla_w_gate, gla_b_gate, gla_norm, dn_conv_w,
           dn_a_log, dn_dt_bias, dn_norm, ab_w_out, mla_w_a, mla_q_norm, mla_kv_norm, mla_w_qb,
           mla_w_kvb, mla_w_o, ffn_w_gate, ffn_w_up, ffn_w_down):
    bp, lp, d_model = x_prompt.shape
    bs, ls, _ = x_sample.shape
    assert ls == 1
    tp = bp * lp
    depth = norm_mix.shape[0]
    past_len = page_table.shape[1] * cache_latent.shape[2]
    gla_dk, gla_dv = state_gla.shape[3:]
    dn_ch = state_conv.shape[-1]

    x = jnp.concatenate([x_prompt.reshape(tp, d_model), x_sample.reshape(bs, d_model)], axis=0)
    pos = jnp.concatenate([jnp.tile(jnp.arange(lp), bp), jnp.full((bs,), past_len)])
    cs = _rope_table(pos)

    cache_k_rope_t = jnp.swapaxes(cache_k_rope, 2, 3)
    gla_p, dn_p, cv_p, cv_s = [], [], [], []
    lat_p, lat_s, kr_p, kr_s = [], [], [], []
    gla_s = dn_s = None
    for layer in range(depth):
        j = layer // 2
        if layer % 2 == 0:
            cols, aligned, n_main = _ab_columns(d_model)
            proj = _ab_in_proj(x, norm_mix[layer], ab_w_in, j, _ab_small_weight(ab_w_in, j, aligned, n_main),
                               aligned=aligned, shift=GLA_GATE_RANK, n_out=n_main)
            small = proj
            wg_pad = jnp.zeros((LANE, gla_w_gate.shape[2]), F32).at[:GLA_GATE_RANK].set(gla_w_gate[j])
            alog_pad = _pad_lanes(dn_a_log[j], cols["a_lane"])
            dtb_pad = _pad_lanes(dn_dt_bias[j], cols["a_lane"])
            mix, g1 = _gla_prompt(proj, small, cols, wg_pad, gla_b_gate[j], gla_norm[j], batch=bp, seq=lp,
                                  dk=gla_dk, dv=gla_dv, mix_shape=(tp + bs, GLA_HEADS * gla_dv + DN_HEADS * DN_HEAD))
            mix, gla_s = _gla_sample(proj, small, cols, wg_pad, gla_b_gate[j], gla_norm[j], state_gla, j, gla_s, mix,
                                     row0=tp, dk=gla_dk, dv=gla_dv)
            mix, d1, c1 = _dn_prompt(proj, small, cols, dn_conv_w, j, alog_pad, dtb_pad, dn_norm[j], mix,
                                     batch=bp, seq=lp)
            mix, dn_s = _dn_sample(proj, small, cols, dn_conv_w, j, alog_pad, dtb_pad, dn_norm[j],
                                   state_delta, state_conv, dn_s, mix, row0=tp)
            c2 = jnp.concatenate([state_conv[j, :, 1:], proj[tp:, None, cols["dn_q"]:cols["dn_q"] + dn_ch]], axis=1)
            x = _matmul(mix, [ab_w_out], tn=512, out_dtype=F32, resid=x, layer=j, name="ab_out_proj")
            gla_p.append(g1)
            dn_p.append(d1)
            cv_p.append(c1); cv_s.append(c2)
        else:
            x, latent, kr2 = _mla_mixer(x, cs, norm_mix[layer], mla_w_a[j], mla_q_norm[j], mla_kv_norm[j],
                                        mla_w_qb[j], mla_w_kvb, mla_w_o, cache_latent, cache_k_rope_t,
                                        page_table, j, bp=bp, lp=lp, bs=bs)
            lat_p.append(latent[:tp].reshape(bp, lp, KV_LORA)); lat_s.append(latent[tp:].reshape(bs, 1, KV_LORA))
            kr_p.append(kr2[:tp, :QK_ROPE].reshape(bp, lp, QK_ROPE)); kr_s.append(kr2[tp:, :QK_ROPE].reshape(bs, 1, QK_ROPE))
        hid = _matmul(x, [ffn_w_gate, ffn_w_up], tn=256, out_dtype=BF16, gain=norm_ffn[layer], layer=layer,
                      name="ffn_gate_up")
        x = _matmul(hid, [ffn_w_down], tn=256, out_dtype=F32, resid=x, layer=layer, name="ffn_down")
    y_prompt = _rmsnorm_rows(x, norm_final, row0=0, n_rows=tp, rows=256).reshape(bp, lp, d_model)
    y_sample = _rmsnorm_rows(x, norm_final, row0=tp, n_rows=bs, rows=bs).reshape(bs, 1, d_model)
    return (y_prompt, y_sample, jnp.stack(gla_p), gla_s, jnp.stack(dn_p), dn_s,
            jnp.stack(cv_p), jnp.stack(cv_s), jnp.stack(lat_p), jnp.stack(lat_s), jnp.stack(kr_p), jnp.stack(kr_s))
```

```python
import functools
import math

import numpy as np
import jax
import jax.numpy as jnp
from jax import lax
from jax.experimental import pallas as pl
from jax.experimental.pallas import tpu as pltpu

F32 = jnp.float32
BF16 = jnp.bfloat16
HIGHEST = lax.Precision.HIGHEST

GLA_HEADS = 4
GLA_GATE_RANK = 16
GLA_GATE_TAU = 16.0
DN_HEADS = 16
DN_HEAD = 128
CONV_WIDTH = 4
CHUNK = 64
MLA_HEADS = 32
Q_LORA = 1024
KV_LORA = 512
QK_NOPE = 128
QK_ROPE = 64
V_HEAD = 128
ROPE_THETA = 10000.0
RMS_EPS = 1e-6
L2_EPS = 1e-6

LANE = 128
VMEM_LIMIT_BYTES = 60 * 1024 * 1024

ROW_TILE = 1040
NORM_ROWS = 16
GLA_SUB = 8
DN_SUB = 16
GLA_STEP_CHUNKS = 8
DN_GROUP = 16
DN_SAMPLE_GROUP = 8
ATT_TQ = 512
DEC_PAGES = 64
SAMPLE_ROWS = 8


def _cparams(sem):
    return pltpu.CompilerParams(dimension_semantics=sem, vmem_limit_bytes=VMEM_LIMIT_BYTES)


def _silu(x):
    return x * jax.nn.sigmoid(x)


def _log_sigmoid(x):
    return jnp.minimum(x, 0.0) - jnp.log(1.0 + jnp.exp(-jnp.abs(x)))


def _softplus(x):
    return jnp.maximum(x, 0.0) + jnp.log(1.0 + jnp.exp(-jnp.abs(x)))


def _nt(a, b):
    return lax.dot_general(a, b, (((1,), (1,)), ((), ())), preferred_element_type=F32)


def _tn(a, b):
    return lax.dot_general(a, b, (((0,), (0,)), ((), ())), preferred_element_type=F32)


def _dotx(a, b):
    return jnp.dot(a, b, precision=HIGHEST, preferred_element_type=F32)


def _stage_lhs(x_ref, g_ref, xs_ref):
    @pl.when(pl.program_id(1) == 0)
    def _():
        def body(r, carry):
            sl = pl.ds(pl.multiple_of(r * NORM_ROWS, NORM_ROWS), NORM_ROWS)
            x = x_ref[sl, :].astype(F32)
            if g_ref is not None:
                x = x * lax.rsqrt(jnp.mean(x * x, axis=-1, keepdims=True) + RMS_EPS) * g_ref[...]
            xs_ref[sl, :] = x.astype(BF16)
            return carry
        trips = x_ref.shape[0] // NORM_ROWS
        lax.fori_loop(0, trips, body, 0, unroll=max(u for u in (5, 4, 2, 1) if trips % u == 0))


def _mm_kernel(*refs, has_norm, has_resid, n_w, stage):
    it = iter(refs)
    x_ref = next(it)
    g_ref = next(it) if has_norm else None
    w_refs = [next(it) for _ in range(n_w)]
    r_ref = next(it) if has_resid else None
    o_ref = next(it)
    xs_ref = next(it) if stage else None

    if stage:
        _stage_lhs(x_ref, g_ref, xs_ref)
        xb = xs_ref[...]
    else:
        xb = x_ref[...]
    acc = [jnp.dot(xb, w[...].astype(BF16), preferred_element_type=F32) for w in w_refs]
    y = _silu(acc[0]) * acc[1] if n_w == 2 else acc[0]
    if has_resid:
        y = y + r_ref[...]
    o_ref[...] = y.astype(o_ref.dtype)


def _matmul(x, ws, *, tn, out_dtype, gain=None, resid=None, layer=None, name):
    m = x.shape[0]
    k, n = ws[0].shape[-2:]
    tm = ROW_TILE
    assert m % tm == 0 and n % tn == 0 and k % LANE == 0 and tm % NORM_ROWS == 0
    stage = gain is not None or x.dtype != BF16
    in_specs = [pl.BlockSpec((tm, k), lambda i, j: (i, 0), pipeline_mode=pl.Buffered(1))]
    args = [x]
    if gain is not None:
        in_specs.append(pl.BlockSpec((1, k), lambda i, j: (0, 0)))
        args.append(gain.reshape(1, k).astype(F32))
    for w in ws:
        if layer is None:
            in_specs.append(pl.BlockSpec((k, tn), lambda i, j: (0, j)))
        else:
            in_specs.append(pl.BlockSpec((None, k, tn), lambda i, j: (layer, 0, j)))
        args.append(w)
    if resid is not None:
        in_specs.append(pl.BlockSpec((tm, tn), lambda i, j: (i, j)))
        args.append(resid)
    scratch = [pltpu.VMEM((tm, k), BF16)] if stage else []
    return pl.pallas_call(
        functools.partial(_mm_kernel, has_norm=gain is not None, has_resid=resid is not None,
                          n_w=len(ws), stage=stage),
        out_shape=jax.ShapeDtypeStruct((m, n), out_dtype),
        grid=(m // tm, n // tn),
        in_specs=in_specs,
        out_specs=pl.BlockSpec((tm, tn), lambda i, j: (i, j)),
        scratch_shapes=scratch,
        compiler_params=_cparams(("parallel", "arbitrary")),
        name=name,
    )(*args)


def _ab_in_kernel(x_ref, g_ref, w_ref, wx_ref, ws_ref, o_ref, xs_ref, *, n_plain, n_main, shift):
    j = pl.program_id(1)
    tm, tn = o_ref.shape
    _stage_lhs(x_ref, g_ref, xs_ref)

    @pl.when(j < n_plain)
    def _():
        o_ref[...] = jnp.dot(xs_ref[...], w_ref[...].astype(BF16), preferred_element_type=F32)

    @pl.when((j >= n_plain) & (j < n_main))
    def _():
        w = jnp.concatenate([w_ref[...], wx_ref[...]], axis=1).astype(BF16)[:, shift:shift + tn]
        o_ref[...] = jnp.dot(xs_ref[...], w, preferred_element_type=F32)

    @pl.when(j == n_main)
    def _():
        y = jnp.dot(xs_ref[...], ws_ref[...].astype(BF16), preferred_element_type=F32)
        o_ref[...] = jnp.concatenate([y, jnp.zeros((tm, tn - y.shape[1]), F32)], axis=1)


def _ab_in_proj(x, gain, w_all, layer, w_small, *, aligned, shift, n_out, tn=512):
    m, k = x.shape
    tm = ROW_TILE
    assert m % tm == 0 and aligned % tn == 0 and n_out % tn == 0 and 0 < shift < LANE
    n_plain = aligned // tn
    n_main = n_out // tn
    sub = tn // LANE
    return pl.pallas_call(
        functools.partial(_ab_in_kernel, n_plain=n_plain, n_main=n_main, shift=shift),
        out_shape=jax.ShapeDtypeStruct((m, n_out + tn), F32),
        grid=(m // tm, n_main + 1),
        in_specs=[pl.BlockSpec((tm, k), lambda i, j: (i, 0), pipeline_mode=pl.Buffered(1)),
                  pl.BlockSpec((1, k), lambda i, j: (0, 0)),
                  pl.BlockSpec((None, k, tn), lambda i, j: (layer, 0, jnp.minimum(j, n_main - 1))),
                  pl.BlockSpec((None, k, LANE), lambda i, j: (layer, 0, sub * (jnp.clip(j, n_plain, n_main - 1) + 1))),
                  pl.BlockSpec((k, LANE), lambda i, j: (0, 0), pipeline_mode=pl.Buffered(1))],
        out_specs=pl.BlockSpec((tm, tn), lambda i, j: (i, j)),
        scratch_shapes=[pltpu.VMEM((tm, k), BF16)],
        compiler_params=_cparams(("parallel", "arbitrary")),
        name="ab_in_proj",
    )(x, gain.reshape(1, k).astype(F32), w_all, w_all, w_small)


def _rmsnorm_kernel(x_ref, g_ref, o_ref):
    x = x_ref[...]
    o_ref[...] = x * lax.rsqrt(jnp.mean(x * x, axis=-1, keepdims=True) + RMS_EPS) * g_ref[...]


def _rmsnorm_rows(x, gain, *, row0, n_rows, rows):
    d = x.shape[1]
    assert n_rows % rows == 0 and row0 % rows == 0
    return pl.pallas_call(
        _rmsnorm_kernel,
        out_shape=jax.ShapeDtypeStruct((n_rows, d), F32),
        grid=(n_rows // rows,),
        in_specs=[pl.BlockSpec((rows, d), lambda i: (row0 // rows + i, 0)), pl.BlockSpec((1, d), lambda i: (0, 0))],
        out_specs=pl.BlockSpec((rows, d), lambda i: (i, 0)),
        compiler_params=_cparams(("parallel",)),
        name="final_rmsnorm",
    )(x, gain.reshape(1, d))


def _gla_gate(sm, wg_ref, bg_ref):
    gp = jnp.dot(sm.astype(BF16), wg_ref[...].astype(BF16), preferred_element_type=F32) + bg_ref[...]
    return _log_sigmoid(gp) / GLA_GATE_TAU


def _gated_head_norm(o, nw_ref, gate):
    on = o * lax.rsqrt(jnp.mean(o * o, axis=-1, keepdims=True) + RMS_EPS) * nw_ref[...]
    return on * _silu(gate)


def _gla_prompt_kernel(q_ref, k_ref, v_ref, go_ref, sm_ref, wg_ref, bg_ref, nw_ref,
                       o_ref, st_ref, sT_ref):
    c = pl.program_id(2)
    dk = q_ref.shape[1]
    C = CHUNK
    sub = GLA_SUB

    @pl.when(c == 0)
    def _():
        sT_ref[...] = jnp.zeros_like(sT_ref)

    r_i = lax.broadcasted_iota(jnp.int32, (C, C), 0)
    c_i = lax.broadcasted_iota(jnp.int32, (C, C), 1)
    tril = (r_i >= c_i).astype(F32)
    key_row = lax.broadcasted_iota(jnp.int32, (C, 1), 0)
    lane = lax.broadcasted_iota(jnp.int32, (sub, C), 1)
    rloc = lax.broadcasted_iota(jnp.int32, (sub, C), 0)

    for cc in range(q_ref.shape[0] // C):
        rows = slice(cc * C, (cc + 1) * C)
        q = q_ref[rows, :] * (dk ** -0.5)
        k = k_ref[rows, :]
        v = v_ref[rows, :].astype(BF16)
        g = _gla_gate(sm_ref[rows, :], wg_ref, bg_ref)
        cum = _dotx(tril, g)
        blocks = []
        for i in range(C // sub):
            r0 = i * sub
            qi = q[r0:r0 + sub]
            cumi = cum[r0:r0 + sub]
            dg = jnp.zeros((sub, C), F32)
            for s in range(r0, r0 + sub):
                x = qi * k[s:s + 1] * jnp.exp(jnp.minimum(cumi - cum[s:s + 1], 0.0))
                dg = jnp.where(lane == s, jnp.sum(x, axis=1, keepdims=True), dg)
            att_i = jnp.where(rloc + r0 >= lane, dg, 0.0)
            if i > 0:
                ci = cum[r0 - 1:r0]
                qt = qi * jnp.exp(cumi - ci)
                kt = jnp.where(key_row < r0, k * jnp.exp(jnp.minimum(ci - cum, 0.0)), 0.0)
                att_i = att_i + _nt(qt.astype(BF16), kt.astype(BF16))
            blocks.append(att_i)
        att = jnp.concatenate(blocks, axis=0)

        sT = sT_ref[...]
        last = cum[C - 1:C]
        o = jnp.dot(att.astype(BF16), v, preferred_element_type=F32)
        o = o + _nt((q * jnp.exp(cum)).astype(BF16), sT.astype(BF16))
        kd = k * jnp.exp(last - cum)
        sT_ref[...] = sT * jnp.exp(last) + _tn(v, kd.astype(BF16))
        o_ref[rows, :] = _gated_head_norm(o, nw_ref, go_ref[rows, :]).astype(o_ref.dtype)

    @pl.when(c == pl.num_programs(2) - 1)
    def _():
        st_ref[0, 0] = sT_ref[...].T


def _gla_prompt(proj, small, cols, wg_pad, b_gate, gla_norm, *, batch, seq, dk, dv, mix_shape):
    step = GLA_STEP_CHUNKS * CHUNK
    nc = seq // step
    assert seq % step == 0
    h = GLA_HEADS
    qb, kb, vb, gb = cols["gq"] // dk, cols["gk"] // dk, cols["gv"] // dv, cols["g_out"] // dv
    row = lambda b, hh, c: b * nc + c
    in_specs = [
        pl.BlockSpec((step, dk), lambda b, hh, c: (row(b, hh, c), qb + hh)),
        pl.BlockSpec((step, dk), lambda b, hh, c: (row(b, hh, c), kb + hh)),
        pl.BlockSpec((step, dv), lambda b, hh, c: (row(b, hh, c), vb + hh)),
        pl.BlockSpec((step, dv), lambda b, hh, c: (row(b, hh, c), gb + hh)),
        pl.BlockSpec((step, LANE), lambda b, hh, c: (row(b, hh, c), cols["small_blk"])),
        pl.BlockSpec((LANE, dk), lambda b, hh, c: (0, hh)),
        pl.BlockSpec((1, dk), lambda b, hh, c: (0, hh)),
        pl.BlockSpec((1, dv), lambda b, hh, c: (0, 0)),
    ]
    return pl.pallas_call(
        _gla_prompt_kernel,
        out_shape=(jax.ShapeDtypeStruct(mix_shape, BF16),
                   jax.ShapeDtypeStruct((batch, h, dk, dv), F32)),
        grid=(batch, h, nc),
        in_specs=in_specs,
        out_specs=(pl.BlockSpec((step, dv), lambda b, hh, c: (row(b, hh, c), hh)),
                   pl.BlockSpec((1, 1, dk, dv), lambda b, hh, c: (b, hh, 0, 0))),
        scratch_shapes=[pltpu.VMEM((dv, dk), F32)],
        compiler_params=_cparams(("parallel", "parallel", "arbitrary")),
        name="gla_prompt",
    )(proj, proj, proj, proj, small, wg_pad, b_gate.reshape(1, -1), gla_norm.reshape(1, -1))


def _alias_outputs(in_specs, args, bufs):
    aliases = {}
    for k, buf in enumerate(bufs):
        if buf is not None:
            aliases[len(args)] = k
            in_specs.append(pl.BlockSpec(memory_space=pl.ANY))
            args.append(buf)
    return aliases


def _gla_sample_kernel(q_ref, k_ref, v_ref, go_ref, sm_ref, wg_ref, bg_ref, nw_ref, s_ref, *rest):
    o_ref, so_ref = rest[-2:]
    nb, dk = q_ref.shape
    qT = (q_ref[...] * (dk ** -0.5)).T
    kT = k_ref[...].T
    egT = jnp.exp(_gla_gate(sm_ref[...], wg_ref, bg_ref)).T
    v = v_ref[...]
    rows = []
    for b in range(nb):
        s_new = s_ref[b, 0] * egT[:, b:b + 1] + kT[:, b:b + 1] * v[b:b + 1]
        so_ref[b, 0] = s_new
        rows.append(jnp.sum(qT[:, b:b + 1] * s_new, axis=0, keepdims=True))
    o = jnp.concatenate(rows, axis=0)
    o_ref[...] = _gated_head_norm(o, nw_ref, go_ref[...]).astype(o_ref.dtype)


def _gla_sample(proj, small, cols, wg_pad, b_gate, gla_norm, states, j, prev, mix, *, row0, dk, dv):
    nb = SAMPLE_ROWS
    bs, h = states.shape[1:3]
    assert bs % nb == 0 and row0 % nb == 0
    r0 = row0 // nb
    qb, kb, vb, gb = cols["gq"] // dk, cols["gk"] // dk, cols["gv"] // dv, cols["g_out"] // dv
    in_specs = [
        pl.BlockSpec((nb, dk), lambda i, hh: (r0 + i, qb + hh)),
        pl.BlockSpec((nb, dk), lambda i, hh: (r0 + i, kb + hh)),
        pl.BlockSpec((nb, dv), lambda i, hh: (r0 + i, vb + hh)),
        pl.BlockSpec((nb, dv), lambda i, hh: (r0 + i, gb + hh)),
        pl.BlockSpec((nb, LANE), lambda i, hh: (r0 + i, cols["small_blk"])),
        pl.BlockSpec((LANE, dk), lambda i, hh: (0, hh)),
        pl.BlockSpec((1, dk), lambda i, hh: (0, hh)),
        pl.BlockSpec((1, dv), lambda i, hh: (0, 0)),
        pl.BlockSpec((None, nb, 1, dk, dv), lambda i, hh: (j, i, hh, 0, 0)),
    ]
    args = [proj, proj, proj, proj, small, wg_pad, b_gate.reshape(1, -1), gla_norm.reshape(1, -1), states]
    aliases = _alias_outputs(in_specs, args, [mix, prev])
    return pl.pallas_call(
        _gla_sample_kernel,
        out_shape=(jax.ShapeDtypeStruct(mix.shape, BF16),
                   jax.ShapeDtypeStruct(states.shape, F32)),
        grid=(bs // nb, h),
        in_specs=in_specs,
        out_specs=(pl.BlockSpec((nb, dv), lambda i, hh: (r0 + i, hh)),
                   pl.BlockSpec((None, nb, 1, dk, dv), lambda i, hh: (j, i, hh, 0, 0))),
        input_output_aliases=aliases,
        compiler_params=_cparams(("parallel", "parallel")),
        name="gla_sample",
    )(*args)


def _dn_gates(sm, alog_ref, dtb_ref):
    decay = -jnp.exp(alog_ref[...]) * _softplus(sm + dtb_ref[...])
    return decay, jax.nn.sigmoid(sm)


def _lane_col(x, lane_idx):
    lane = lax.broadcasted_iota(jnp.int32, x.shape, 1)
    return jnp.sum(jnp.where(lane == lane_idx, x, 0.0), axis=1, keepdims=True)


def _l2norm(x):
    return x * lax.rsqrt(jnp.sum(x * x, axis=-1, keepdims=True) + L2_EPS)


def _split_bf16(x):
    hi = x.astype(BF16)
    return hi, (x - hi.astype(F32)).astype(BF16)


def _bmm3(a, b):
    ah, al = _split_bf16(a)
    bh, bl = _split_bf16(b)
    mm = lambda x, y: jnp.einsum('gij,gjk->gik', x, y, preferred_element_type=F32)
    return mm(ah, bh) + (mm(ah, bl) + mm(al, bh))


def _unit_lower_inverse(a, sub):
    C = a.shape[-1]
    assert C // sub == 4
    r = lax.broadcasted_iota(jnp.int32, (C, C), 0)
    c = lax.broadcasted_iota(jnp.int32, (C, C), 1)
    eye = (r == c).astype(F32)
    same_blk = (r // sub) == (c // sub)
    x = jnp.where(same_blk, -a, 0.0)
    low = jnp.where(same_blk, 0.0, a)
    td = eye + x
    p = x
    for _ in range(int(math.log2(sub)) - 1):
        p = _bmm3(p, p)
        td = _bmm3(td, eye + p)
    m = _bmm3(td, low)
    tm = _bmm3(eye - m, eye + _bmm3(m, m))
    return _bmm3(tm, td)


def _heads(x, n):
    return jnp.stack([x[:, h * DN_HEAD:(h + 1) * DN_HEAD] for h in range(n)], axis=0)


def _unheads(x):
    return jnp.concatenate([x[h] for h in range(x.shape[0])], axis=1)


def _dn_prep_kernel(q_ref, k_ref, v_ref, sm_ref, wq_ref, wk_ref, wv_ref, alog_ref, dtb_ref,
                    u_ref, w_ref, qe_ref, kd_ref, att_ref, cum_ref, cq_ref, ck_ref, cv_ref, prev_ref,
                    *, a_lane, b_lane):
    hg = pl.program_id(1)
    c = pl.program_id(2)
    C = q_ref.shape[0]
    d = DN_HEAD
    G = q_ref.shape[1] // d

    @pl.when(c == 0)
    def _():
        prev_ref[...] = jnp.zeros_like(prev_ref)

    def conv(x_ref, w_ref_, slot):
        x = x_ref[...]
        ext = jnp.concatenate([prev_ref[slot], x], axis=0)
        y = x * w_ref_[CONV_WIDTH - 1:CONV_WIDTH]
        for i in range(1, CONV_WIDTH):
            y = y + pltpu.roll(ext, i, 0)[8:8 + C] * w_ref_[CONV_WIDTH - 1 - i:CONV_WIDTH - i]
        prev_ref[slot] = x[C - 8:C]
        return _silu(y)

    q3 = _heads(conv(q_ref, wq_ref, 0), G)
    k3 = _heads(conv(k_ref, wk_ref, 1), G)
    v2 = _heads(conv(v_ref, wv_ref, 2), G)
    q2 = _l2norm(q3) * (d ** -0.5)
    k2 = _l2norm(k3)

    decay, beta_all = _dn_gates(sm_ref[...], alog_ref, dtb_ref)
    r_i = lax.broadcasted_iota(jnp.int32, (C, C), 0)
    c_i = lax.broadcasted_iota(jnp.int32, (C, C), 1)
    cum_all = _dotx((r_i >= c_i).astype(F32), decay)
    cum_ref[...] = cum_all
    r_w = lax.broadcasted_iota(jnp.int32, (C, LANE), 0)
    c_w = lax.broadcasted_iota(jnp.int32, (C, LANE), 1)
    cums, betas, cum_rows = [], [], []
    for hl in range(G):
        head = hg * G + hl
        cum_h = _lane_col(cum_all, a_lane + head)
        cums.append(cum_h)
        betas.append(_lane_col(beta_all, b_lane + head))
        cum_rows.append(jnp.sum(jnp.where(r_w == c_w, cum_h, 0.0), axis=0, keepdims=True))
    cum = jnp.stack(cums, axis=0)
    beta = jnp.stack(betas, axis=0)
    cum_row = jnp.stack(cum_rows, axis=0)
    dec_w = jnp.exp(jnp.minimum(cum - cum_row, 0.0))
    dec = dec_w[:, :, :C]

    kbeta = k2 * beta
    k2b = k2.astype(BF16)
    kk = jnp.einsum('gtd,gsd->gts', kbeta.astype(BF16), k2b, preferred_element_type=F32)
    t = _unit_lower_inverse(jnp.where(r_i > c_i, kk * dec, 0.0), DN_SUB)
    ecum = jnp.exp(cum)
    sol = _bmm3(t, jnp.concatenate([v2 * beta, kbeta * ecum], axis=-1))
    k_wide = jnp.concatenate([k2b, jnp.zeros_like(k2b)], axis=1)
    qk = jnp.einsum('gtd,gsd->gts', q2.astype(BF16), k_wide, preferred_element_type=F32)
    att = jnp.where(r_w >= c_w, qk * dec_w, 0.0)
    last = cum[:, C - 1:C]

    u_ref[...] = _unheads(sol[:, :, :d])
    w_ref[...] = _unheads(sol[:, :, d:]).astype(BF16)
    qe_ref[...] = _unheads(q2 * ecum).astype(BF16)
    kd_ref[...] = _unheads(k2 * jnp.exp(last - cum)).astype(BF16)
    att_ref[...] = _unheads(att).astype(BF16)

    @pl.when(c == pl.num_programs(2) - 1)
    def _():
        cq_ref[0] = prev_ref[0]
        ck_ref[0] = prev_ref[1]
        cv_ref[0] = prev_ref[2]


def _dn_scan_kernel(u_ref, w_ref, qe_ref, kd_ref, att_ref, cum_ref, z_ref, nw_ref, *rest, a_lane, group):
    o_ref, st_ref, s_ref = rest[-3:]
    c = pl.program_id(1)
    C = u_ref.shape[0]
    H = DN_HEADS

    @pl.when(c == 0)
    def _():
        s_ref[...] = jnp.zeros_like(s_ref)

    s = s_ref[...]
    sb = s.astype(BF16)
    v_new = _heads(u_ref[...], H) - jnp.einsum('hcd,hdv->hcv', _heads(w_ref[...], H), sb,
                                               preferred_element_type=F32)
    vb = v_new.astype(BF16)
    o = jnp.einsum('hts,hsv->htv', _heads(att_ref[...], H)[:, :, :C], vb, preferred_element_type=F32)
    o = o + jnp.einsum('hcd,hdv->hcv', _heads(qe_ref[...], H), sb, preferred_element_type=F32)
    cum_last = cum_ref[C - 1:C, :]
    lanes = [(h // group) * LANE + a_lane + h for h in range(H)]
    e_last = jnp.exp(jnp.stack([cum_last[:, l:l + 1] for l in lanes], axis=0))
    s_new = s * e_last + jnp.einsum('hcd,hcv->hdv', _heads(kd_ref[...], H), vb, preferred_element_type=F32)
    s_ref[...] = s_new
    on = o * lax.rsqrt(jnp.mean(o * o, axis=-1, keepdims=True) + RMS_EPS) * nw_ref[...]
    o_ref[...] = _unheads(on * _silu(_heads(z_ref[...], H))).astype(o_ref.dtype)

    @pl.when(c == pl.num_programs(1) - 1)
    def _():
        st_ref[0] = s_new


def _dn_prompt(proj, small, cols, conv_w, j, alog_pad, dtb_pad, dn_norm, mix, *, batch, seq):
    nc = seq // CHUNK
    assert seq % CHUNK == 0
    G, d = DN_GROUP, DN_HEAD
    gw = G * d
    ng = DN_HEADS // G
    dn_k = DN_HEADS * d
    t = batch * seq
    qb, kb, vb = cols["dn_q"] // gw, cols["dn_k"] // gw, cols["dn_v"] // gw
    row = lambda b, g, c: b * nc + c
    wide = lambda dt: jax.ShapeDtypeStruct((t, dn_k), dt)
    tok = pl.BlockSpec((CHUNK, gw), lambda b, g, c: (row(b, g, c), g))
    tail = pl.BlockSpec((1, 8, gw), lambda b, g, c: (b, 0, g))
    u, w, qe, kd, att, cum, cq, ck, cv = pl.pallas_call(
        functools.partial(_dn_prep_kernel, a_lane=cols["a_lane"], b_lane=cols["b_lane"]),
        out_shape=(wide(F32), wide(BF16), wide(BF16), wide(BF16), wide(BF16),
                   jax.ShapeDtypeStruct((t, ng * LANE), F32),
                   jax.ShapeDtypeStruct((batch, 8, dn_k), F32), jax.ShapeDtypeStruct((batch, 8, dn_k), F32),
                   jax.ShapeDtypeStruct((batch, 8, dn_k), F32)),
        grid=(batch, ng, nc),
        in_specs=[
            pl.BlockSpec((CHUNK, gw), lambda b, g, c: (row(b, g, c), qb + g)),
            pl.BlockSpec((CHUNK, gw), lambda b, g, c: (row(b, g, c), kb + g)),
            pl.BlockSpec((CHUNK, gw), lambda b, g, c: (row(b, g, c), vb + g)),
            pl.BlockSpec((CHUNK, LANE), lambda b, g, c: (row(b, g, c), cols["small_blk"])),
            pl.BlockSpec((None, CONV_WIDTH, gw), lambda b, g, c: (j, 0, g)),
            pl.BlockSpec((None, CONV_WIDTH, gw), lambda b, g, c: (j, 0, dn_k // gw + g)),
            pl.BlockSpec((None, CONV_WIDTH, gw), lambda b, g, c: (j, 0, 2 * dn_k // gw + g)),
            pl.BlockSpec((1, LANE), lambda b, g, c: (0, 0)),
            pl.BlockSpec((1, LANE), lambda b, g, c: (0, 0)),
        ],
        out_specs=(tok, tok, tok, tok, tok,
                   pl.BlockSpec((CHUNK, LANE), lambda b, g, c: (row(b, g, c), g)), tail, tail, tail),
        scratch_shapes=[pltpu.VMEM((3, 8, gw), F32)],
        compiler_params=_cparams(("parallel", "parallel", "arbitrary")),
        name="deltanet_prep",
    )(proj, proj, proj, small, conv_w, conv_w, conv_w, alog_pad, dtb_pad)

    full = lambda width: pl.BlockSpec((CHUNK, width), lambda b, c: (b * nc + c, 0))
    in_specs = [full(dn_k), full(dn_k), full(dn_k), full(dn_k), full(dn_k), full(ng * LANE),
                pl.BlockSpec((CHUNK, dn_k), lambda b, c: (b * nc + c, cols["dn_z"] // dn_k)),
                pl.BlockSpec((1, d), lambda b, c: (0, 0))]
    args = [u, w, qe, kd, att, cum, proj, dn_norm.reshape(1, -1)]
    aliases = _alias_outputs(in_specs, args, [mix])
    mix, st = pl.pallas_call(
        functools.partial(_dn_scan_kernel, a_lane=cols["a_lane"], group=G),
        out_shape=(jax.ShapeDtypeStruct(mix.shape, BF16),
                   jax.ShapeDtypeStruct((batch, DN_HEADS, d, d), F32)),
        grid=(batch, nc),
        in_specs=in_specs,
        out_specs=(pl.BlockSpec((CHUNK, dn_k), lambda b, c: (b * nc + c, mix.shape[1] // dn_k - 1)),
                   pl.BlockSpec((1, DN_HEADS, d, d), lambda b, c: (b, 0, 0, 0))),
        scratch_shapes=[pltpu.VMEM((DN_HEADS, d, d), F32)],
        input_output_aliases=aliases,
        compiler_params=_cparams(("parallel", "arbitrary")),
        name="deltanet_scan",
    )(*args)
    tail_rows = slice(8 - (CONV_WIDTH - 1), 8)
    conv_new = jnp.concatenate([cq[:, tail_rows], ck[:, tail_rows], cv[:, tail_rows]], axis=-1)
    return mix, st, conv_new


def _dn_sample_kernel(q_ref, k_ref, v_ref, z_ref, sm_ref, bq_ref, bk_ref, bv_ref, wq_ref, wk_ref, wv_ref,
                      alog_ref, dtb_ref, nw_ref, s_ref, *rest, a_lane, b_lane):
    o_ref, so_ref = rest[-2:]
    hg = pl.program_id(1)
    nb = q_ref.shape[0]
    G = s_ref.shape[1]
    d = DN_HEAD

    def conv(x_ref, buf_ref, w_ref):
        y = x_ref[...] * w_ref[CONV_WIDTH - 1:CONV_WIDTH]
        for i in range(CONV_WIDTH - 1):
            y = y + buf_ref[:, i, :] * w_ref[i:i + 1]
        return _silu(y)

    yq = conv(q_ref, bq_ref, wq_ref)
    yk = conv(k_ref, bk_ref, wk_ref)
    yv = conv(v_ref, bv_ref, wv_ref)
    decay, beta_all = _dn_gates(sm_ref[...], alog_ref, dtb_ref)
    z = z_ref[...]
    outs = []
    for hl in range(G):
        head = hg * G + hl
        sl = slice(hl * d, (hl + 1) * d)
        q2 = _l2norm(yq[:, sl]) * (d ** -0.5)
        k2 = _l2norm(yk[:, sl])
        v2 = yv[:, sl]
        eg = jnp.exp(_lane_col(decay, a_lane + head))
        beta = _lane_col(beta_all, b_lane + head)
        qT = q2.T
        kT = k2.T
        rows = []
        for b in range(nb):
            s = s_ref[b, hl]
            kc = kT[:, b:b + 1]
            ks = jnp.sum(kc * s, axis=0, keepdims=True)
            v_new = beta[b:b + 1] * (v2[b:b + 1] - eg[b:b + 1] * ks)
            s_new = s * eg[b:b + 1] + kc * v_new
            so_ref[b, hl] = s_new
            rows.append(jnp.sum(qT[:, b:b + 1] * s_new, axis=0, keepdims=True))
        outs.append(_gated_head_norm(jnp.concatenate(rows, axis=0), nw_ref, z[:, sl]))
    o_ref[...] = jnp.concatenate(outs, axis=1).astype(o_ref.dtype)


def _dn_sample(proj, small, cols, conv_w, j, alog_pad, dtb_pad, dn_norm, states, conv_bufs, prev, mix, *, row0):
    nb = SAMPLE_ROWS
    bs = states.shape[1]
    G, d = DN_SAMPLE_GROUP, DN_HEAD
    gw = G * d
    ng = DN_HEADS // G
    dn_k = DN_HEADS * d
    assert bs % nb == 0 and row0 % nb == 0
    r0 = row0 // nb
    qb, kb, vb, zb = cols["dn_q"] // gw, cols["dn_k"] // gw, cols["dn_v"] // gw, cols["dn_z"] // gw
    ob = (mix.shape[1] - dn_k) // gw
    nbuf = CONV_WIDTH - 1
    in_specs = [
        pl.BlockSpec((nb, gw), lambda i, g: (r0 + i, qb + g)),
        pl.BlockSpec((nb, gw), lambda i, g: (r0 + i, kb + g)),
        pl.BlockSpec((nb, gw), lambda i, g: (r0 + i, vb + g)),
        pl.BlockSpec((nb, gw), lambda i, g: (r0 + i, zb + g)),
        pl.BlockSpec((nb, LANE), lambda i, g: (r0 + i, cols["small_blk"])),
        pl.BlockSpec((None, nb, nbuf, gw), lambda i, g: (j, i, 0, g)),
        pl.BlockSpec((None, nb, nbuf, gw), lambda i, g: (j, i, 0, dn_k // gw + g)),
        pl.BlockSpec((None, nb, nbuf, gw), lambda i, g: (j, i, 0, 2 * dn_k // gw + g)),
        pl.BlockSpec((None, CONV_WIDTH, gw), lambda i, g: (j, 0, g)),
        pl.BlockSpec((None, CONV_WIDTH, gw), lambda i, g: (j, 0, dn_k // gw + g)),
        pl.BlockSpec((None, CONV_WIDTH, gw), lambda i, g: (j, 0, 2 * dn_k // gw + g)),
        pl.BlockSpec((1, LANE), lambda i, g: (0, 0)),
        pl.BlockSpec((1, LANE), lambda i, g: (0, 0)),
        pl.BlockSpec((1, d), lambda i, g: (0, 0)),
        pl.BlockSpec((None, nb, G, d, d), lambda i, g: (j, i, g, 0, 0)),
    ]
    args = [proj, proj, proj, proj, small, conv_bufs, conv_bufs, conv_bufs, conv_w, conv_w, conv_w,
            alog_pad, dtb_pad, dn_norm.reshape(1, -1), states]
    aliases = _alias_outputs(in_specs, args, [mix, prev])
    return pl.pallas_call(
        functools.partial(_dn_sample_kernel, a_lane=cols["a_lane"], b_lane=cols["b_lane"]),
        out_shape=(jax.ShapeDtypeStruct(mix.shape, BF16),
                   jax.ShapeDtypeStruct(states.shape, F32)),
        grid=(bs // nb, ng),
        in_specs=in_specs,
        out_specs=(pl.BlockSpec((nb, gw), lambda i, g: (r0 + i, ob + g)),
                   pl.BlockSpec((None, nb, G, d, d), lambda i, g: (j, i, g, 0, 0))),
        input_output_aliases=aliases,
        compiler_params=_cparams(("parallel", "parallel")),
        name="deltanet_sample",
    )(*args)


def _mla_post_kernel(ckv_ref, kpe_ref, cs_ref, g_ref, lat_ref, kr_ref):
    x = ckv_ref[...]
    lat_ref[...] = x * lax.rsqrt(jnp.mean(x * x, axis=-1, keepdims=True) + RMS_EPS) * g_ref[...]
    zz = kpe_ref[...] * cs_ref[...]
    kr_ref[...] = zz + pltpu.roll(zz, QK_ROPE, 1)


def _mla_post(a, cs, kv_norm, *, rows=520):
    m = a.shape[0]
    assert m % rows == 0
    ckv_blk = Q_LORA // KV_LORA
    kpe_blk = (Q_LORA + KV_LORA) // LANE
    return pl.pallas_call(
        _mla_post_kernel,
        out_shape=(jax.ShapeDtypeStruct((m, KV_LORA), F32), jax.ShapeDtypeStruct((m, LANE), F32)),
        grid=(m // rows,),
        in_specs=[pl.BlockSpec((rows, KV_LORA), lambda i: (i, ckv_blk)),
                  pl.BlockSpec((rows, LANE), lambda i: (i, kpe_blk)),
                  pl.BlockSpec((rows, LANE), lambda i: (i, 0)),
                  pl.BlockSpec((1, KV_LORA), lambda i: (0, 0))],
        out_specs=(pl.BlockSpec((rows, KV_LORA), lambda i: (i, 0)),
                   pl.BlockSpec((rows, LANE), lambda i: (i, 0))),
        compiler_params=_cparams(("parallel",)),
        name="mla_latent_rope",
    )(a, a, cs, kv_norm.reshape(1, -1))


def _mla_prefill_kernel(qn_ref, qr_ref, cs_ref, kn_ref, v_ref, kr_ref, o_ref, *, scale, tile):
    n = qn_ref.shape[0] // tile
    kf = jnp.concatenate([kn_ref[...], kr_ref[...].astype(BF16)], axis=1)
    qloc = lax.broadcasted_iota(jnp.int32, (tile, tile), 0)
    kloc = lax.broadcasted_iota(jnp.int32, (tile, tile), 1)
    for qi in range(n):
        rows = slice(qi * tile, (qi + 1) * tile)
        q = jnp.concatenate([(qn_ref[rows, :] * scale).astype(BF16),
                             (qr_ref[rows, :] * cs_ref[rows, :] * scale).astype(BF16)], axis=1)
        for kj in range(qi + 1):
            ks = slice(kj * tile, (kj + 1) * tile)
            s = _nt(q, kf[ks])
            if kj == qi:
                s = jnp.where(kloc <= qloc, s, -jnp.inf)
            s_max = jnp.max(s, axis=-1, keepdims=True)
            if kj == 0:
                m = s_max
                p = jnp.exp(s - m)
                l = jnp.sum(p, axis=-1, keepdims=True)
                acc = jnp.dot(p.astype(BF16), v_ref[ks, :], preferred_element_type=F32)
            else:
                m_new = jnp.maximum(m, s_max)
                corr = jnp.exp(m - m_new)
                p = jnp.exp(s - m_new)
                l = l * corr + jnp.sum(p, axis=-1, keepdims=True)
                acc = acc * corr + jnp.dot(p.astype(BF16), v_ref[ks, :], preferred_element_type=F32)
                m = m_new
        o_ref[rows, :] = (acc / l).astype(o_ref.dtype)


def _mla_prefill(q, cs, kv, kr2, *, batch, seq, out_rows):
    assert seq % ATT_TQ == 0
    h = MLA_HEADS
    scale = (QK_NOPE + QK_ROPE) ** -0.5
    return pl.pallas_call(
        functools.partial(_mla_prefill_kernel, scale=scale, tile=ATT_TQ),
        out_shape=jax.ShapeDtypeStruct((out_rows, h * V_HEAD), BF16),
        grid=(batch, h),
        in_specs=[
            pl.BlockSpec((seq, QK_NOPE), lambda b, hh: (b, hh)),
            pl.BlockSpec((seq, LANE), lambda b, hh: (b, h + hh)),
            pl.BlockSpec((seq, LANE), lambda b, hh: (b, 0)),
            pl.BlockSpec((seq, QK_NOPE), lambda b, hh: (b, 2 * hh)),
            pl.BlockSpec((seq, V_HEAD), lambda b, hh: (b, 2 * hh + 1)),
            pl.BlockSpec((seq, LANE), lambda b, hh: (b, 0)),
        ],
        out_specs=pl.BlockSpec((seq, V_HEAD), lambda b, hh: (b, hh)),
        compiler_params=_cparams(("parallel", "parallel")),
        name="mla_prefill",
    )(q, q, cs, kv, kv, kr2)


def _absorb_q_kernel(x_ref, w_ref, o_ref):
    o_ref[...] = _nt(x_ref[...].astype(BF16), w_ref[...].astype(BF16)).astype(o_ref.dtype)


def _absorb_q(q, w_kvb, j, *, row0, rows):
    h = MLA_HEADS
    assert row0 % rows == 0
    return pl.pallas_call(
        _absorb_q_kernel,
        out_shape=jax.ShapeDtypeStruct((rows, h * KV_LORA), BF16),
        grid=(h,),
        in_specs=[pl.BlockSpec((rows, QK_NOPE), lambda hh: (row0 // rows, hh)),
                  pl.BlockSpec((None, KV_LORA, QK_NOPE), lambda hh: (j, 0, 2 * hh))],
        out_specs=pl.BlockSpec((rows, KV_LORA), lambda hh: (0, hh)),
        compiler_params=_cparams(("parallel",)),
        name="mla_absorb_q",
    )(q, w_kvb)


def _absorb_v_kernel(x_ref, w_ref, *rest):
    o_ref = rest[-1]
    o_ref[...] = jnp.dot(x_ref[...].astype(BF16), w_ref[...].astype(BF16),
                         preferred_element_type=F32).astype(o_ref.dtype)


def _absorb_v(o_lat, w_kvb, j, att, *, row0):
    rows = o_lat.shape[0]
    h = MLA_HEADS
    assert row0 % rows == 0
    in_specs = [pl.BlockSpec((rows, KV_LORA), lambda hh: (0, hh)),
                pl.BlockSpec((None, KV_LORA, V_HEAD), lambda hh: (j, 0, 2 * hh + 1))]
    args = [o_lat, w_kvb]
    aliases = _alias_outputs(in_specs, args, [att])
    return pl.pallas_call(
        _absorb_v_kernel,
        out_shape=jax.ShapeDtypeStruct(att.shape, BF16),
        grid=(h,),
        in_specs=in_specs,
        out_specs=pl.BlockSpec((rows, V_HEAD), lambda hh: (row0 // rows, hh)),
        input_output_aliases=aliases,
        compiler_params=_cparams(("parallel",)),
        name="mla_absorb_v",
    )(*args)


def _mla_decode_kernel(pt_ref, ql_ref, qr_ref, cs_ref, cn_ref, rn_ref, lat_hbm, rope_hbm, o_ref,
                       lat_buf, rope_buf, sem, m_ref, l_ref, acc_ref, *, scale, n_pages, layer):
    g = pl.program_id(1)
    n_groups = pl.num_programs(1)
    t = pl.program_id(0) * n_groups + g
    slot = t % 2

    def page_copies(step, sl):
        out = []
        for i in range(n_pages):
            pid = pt_ref[step * n_pages + i]
            out.append(pltpu.make_async_copy(lat_hbm.at[layer, pid], lat_buf.at[sl, i], sem.at[sl, 0]))
            out.append(pltpu.make_async_copy(rope_hbm.at[layer, pid], rope_buf.at[sl, i], sem.at[sl, 1]))
        return out

    @pl.when(t == 0)
    def _():
        for cp in page_copies(t, slot):
            cp.start()

    @pl.when(t + 1 < pl.num_programs(0) * n_groups)
    def _():
        for cp in page_copies(t + 1, 1 - slot):
            cp.start()

    for cp in page_copies(t, slot):
        cp.wait()
    lat_refs = [lat_buf.at[slot, i] for i in range(n_pages)]
    rope_refs = [rope_buf.at[slot, i] for i in range(n_pages)]
    ql = ql_ref[0]
    zz = qr_ref[0] * cs_ref[...]

    @pl.when(g == 0)
    def _():
        s0 = (jnp.sum(ql.astype(F32) * cn_ref[0], axis=-1, keepdims=True)
              + jnp.sum(zz * rn_ref[0], axis=-1, keepdims=True)) * scale
        m_ref[...] = s0
        l_ref[...] = jnp.ones_like(l_ref)
        acc_ref[...] = jnp.broadcast_to(cn_ref[0], acc_ref.shape)

    qrot = (zz + pltpu.roll(zz, QK_ROPE, 1))[:, :QK_ROPE].astype(BF16)
    cbs = [lat_refs[i][...].astype(BF16) for i in range(n_pages)]
    s = jnp.concatenate(
        [_nt(ql, cbs[i]) + jnp.dot(qrot, rope_refs[i][...].astype(BF16), preferred_element_type=F32)
         for i in range(n_pages)], axis=1) * scale
    m = m_ref[...]
    m_new = jnp.maximum(m, jnp.max(s, axis=-1, keepdims=True))
    corr = jnp.exp(m - m_new)
    e = jnp.exp(s - m_new)
    l_ref[...] = l_ref[...] * corr + jnp.sum(e, axis=-1, keepdims=True)
    eb = e.astype(BF16)
    page = cbs[0].shape[0]
    pv = jnp.dot(eb[:, :page], cbs[0], preferred_element_type=F32)
    for i in range(1, n_pages):
        pv = pv + jnp.dot(eb[:, i * page:(i + 1) * page], cbs[i], preferred_element_type=F32)
    acc_ref[...] = acc_ref[...] * corr + pv
    m_ref[...] = m_new

    @pl.when(g == pl.num_programs(1) - 1)
    def _():
        o_ref[0] = acc_ref[...] / l_ref[...]


def _mla_decode(q_lat, q_rope, cs_row, lat_new, kr_new, cache_latent, cache_k_rope_t, page_table, layer):
    bs = q_lat.shape[0]
    h = MLA_HEADS
    n_log = page_table.shape[1]
    page = cache_latent.shape[2]
    npg = DEC_PAGES
    assert n_log % npg == 0
    scale = (QK_NOPE + QK_ROPE) ** -0.5

    in_specs = [
        pl.BlockSpec((1, h, KV_LORA), lambda b, g, pt: (b, 0, 0)),
        pl.BlockSpec((1, h, LANE), lambda b, g, pt: (b, 0, 0)),
        pl.BlockSpec((1, LANE), lambda b, g, pt: (0, 0)),
        pl.BlockSpec((1, 1, KV_LORA), lambda b, g, pt: (b, 0, 0)),
        pl.BlockSpec((1, 1, LANE), lambda b, g, pt: (b, 0, 0)),
        pl.BlockSpec(memory_space=pl.ANY),
        pl.BlockSpec(memory_space=pl.ANY),
    ]
    grid_spec = pltpu.PrefetchScalarGridSpec(
        num_scalar_prefetch=1,
        grid=(bs, n_log // npg),
        in_specs=in_specs,
        out_specs=pl.BlockSpec((1, h, KV_LORA), lambda b, g, pt: (b, 0, 0)),
        scratch_shapes=[pltpu.VMEM((2, npg, page, KV_LORA), F32), pltpu.VMEM((2, npg, QK_ROPE, page), F32),
                        pltpu.SemaphoreType.DMA((2, 2)),
                        pltpu.VMEM((h, 1), F32), pltpu.VMEM((h, 1), F32), pltpu.VMEM((h, KV_LORA), F32)],
    )
    return pl.pallas_call(
        functools.partial(_mla_decode_kernel, scale=scale, n_pages=npg, layer=layer),
        out_shape=jax.ShapeDtypeStruct((bs, h, KV_LORA), F32),
        grid_spec=grid_spec,
        compiler_params=_cparams(("arbitrary", "arbitrary")),
        name="mla_decode",
    )(page_table.reshape(-1), q_lat.reshape(bs, h, KV_LORA), q_rope.reshape(bs, h, LANE), cs_row,
      lat_new.reshape(bs, 1, KV_LORA), kr_new.reshape(bs, 1, LANE), cache_latent, cache_k_rope_t)


def _rot_cols(w):
    half = QK_ROPE // 2
    return jnp.concatenate([-w[..., half:], w[..., :half]], axis=-1)


def _ab_columns(d_model):
    gla_k = GLA_HEADS * (d_model // 16)
    gla_v = GLA_HEADS * (d_model // 8)
    dn = DN_HEADS * DN_HEAD
    aligned = 2 * gla_k + gla_v
    n_main = aligned + gla_v + 4 * dn
    cols = dict(gq=0, gk=gla_k, gv=2 * gla_k, g_out=aligned, dn_q=aligned + gla_v, dn_k=aligned + gla_v + dn,
                dn_v=aligned + gla_v + 2 * dn, dn_z=aligned + gla_v + 3 * dn, small_blk=n_main // LANE,
                a_lane=GLA_GATE_RANK, b_lane=GLA_GATE_RANK + DN_HEADS)
    return cols, aligned, n_main


def _ab_small_weight(w_all, j, aligned, n_main):
    lo = w_all[j, :, aligned:aligned + GLA_GATE_RANK]
    hi = w_all[j, :, n_main + GLA_GATE_RANK:]
    pad = jnp.zeros((lo.shape[0], LANE - GLA_GATE_RANK - hi.shape[1]), lo.dtype)
    return jnp.concatenate([lo, hi, pad], axis=1)


def _pad_lanes(v, lane0):
    return jnp.zeros((1, LANE), F32).at[0, lane0:lane0 + v.shape[0]].set(v.astype(F32))


def _rope_table(pos):
    half = QK_ROPE // 2
    inv = ROPE_THETA ** (-jnp.arange(half, dtype=F32) / half)
    ang = pos.astype(F32)[:, None] * inv[None, :]
    cos, sin = jnp.cos(ang), jnp.sin(ang)
    return jnp.concatenate([cos, cos, sin, sin], axis=1)


def _mla_mixer(x, cs, norm_w, w_a, q_norm, kv_norm, w_qb, w_kvb, w_o, cache_latent, cache_k_rope_t,
               page_table, j, *, bp, lp, bs):
    tp = bp * lp
    d_model = x.shape[1]
    kpe_w = w_a[:, Q_LORA + KV_LORA:]
    a_pad = jnp.zeros((d_model, -(Q_LORA + KV_LORA + LANE) % 512), F32)
    w_a2 = jnp.concatenate([w_a[:, :Q_LORA + KV_LORA], kpe_w, _rot_cols(kpe_w), a_pad], axis=1)
    a = _matmul(x, [w_a2], tn=512, out_dtype=F32, gain=norm_w, name="mla_a_proj")
    latent, kr2 = _mla_post(a, cs, kv_norm)
    w_qb = w_qb.reshape(Q_LORA, MLA_HEADS, QK_NOPE + QK_ROPE)
    w_rope = w_qb[..., QK_NOPE:]
    w_q2 = jnp.concatenate(
        [w_qb[..., :QK_NOPE].reshape(Q_LORA, -1),
         jnp.concatenate([w_rope, _rot_cols(w_rope)], axis=-1).reshape(Q_LORA, -1)], axis=1)
    q = _matmul(a, [w_q2], tn=2048, out_dtype=F32, gain=q_norm, name="mla_q_proj")
    kv = _matmul(latent, [w_kvb], tn=2048, out_dtype=BF16, layer=j, name="mla_kv_proj")
    att = _mla_prefill(q, cs, kv, kr2, batch=bp, seq=lp, out_rows=x.shape[0])
    q_lat = _absorb_q(q, w_kvb, j, row0=tp, rows=bs)
    q_rope_s = q[tp:, MLA_HEADS * QK_NOPE:]
    o_lat = _mla_decode(q_lat, q_rope_s, cs[tp:tp + 1], latent[tp:], kr2[tp:],
                        cache_latent, cache_k_rope_t, page_table, j)
    att = _absorb_v(o_lat.reshape(bs, MLA_HEADS * KV_LORA), w_kvb, j, att, row0=tp)
    x = _matmul(att, [w_o], tn=512, out_dtype=F32, resid=x, layer=j, name="mla_out_proj")
    return x, latent, kr2


def kernel(x_prompt, x_sample, state_gla, state_delta, state_conv, cache_latent, cache_k_rope, page_table,
           norm_mix, norm_ffn, norm_final, ab_w_in, gla_w_gate, gla_b_gate, gla_norm, dn_conv_w,
           dn_a_log, dn_dt_bias, dn_norm, ab_w_out, mla_w_a, mla_q_norm, mla_kv_norm, mla_w_qb,
           mla_w_kvb, mla_w_o, ffn_w_gate, ffn_w_up, ffn_w_down):
    bp, lp, d_model = x_prompt.shape
    bs, ls, _ = x_sample.shape
    assert ls == 1
    tp = bp * lp
    depth = norm_mix.shape[0]
    past_len = page_table.shape[1] * cache_latent.shape[2]
    gla_dk, gla_dv = state_gla.shape[3:]
    dn_ch = state_conv.shape[-1]

    x = jnp.concatenate([x_prompt.reshape(tp, d_model), x_sample.reshape(bs, d_model)], axis=0)
    pos = jnp.concatenate([jnp.tile(jnp.arange(lp), bp), jnp.full((bs,), past_len)])
    cs = _rope_table(pos)

    cache_k_rope_t = jnp.swapaxes(cache_k_rope, 2, 3)
    gla_p, dn_p, cv_p, cv_s = [], [], [], []
    lat_p, lat_s, kr_p, kr_s = [], [], [], []
    gla_s = dn_s = None
    for layer in range(depth):
        j = layer // 2
        if layer % 2 == 0:
            cols, aligned, n_main = _ab_columns(d_model)
            proj = _ab_in_proj(x, norm_mix[layer], ab_w_in, j, _ab_small_weight(ab_w_in, j, aligned, n_main),
                               aligned=aligned, shift=GLA_GATE_RANK, n_out=n_main)
            small = proj
            wg_pad = jnp.zeros((LANE, gla_w_gate.shape[2]), F32).at[:GLA_GATE_RANK].set(gla_w_gate[j])
            alog_pad = _pad_lanes(dn_a_log[j], cols["a_lane"])
            dtb_pad = _pad_lanes(dn_dt_bias[j], cols["a_lane"])
            mix, g1 = _gla_prompt(proj, small, cols, wg_pad, gla_b_gate[j], gla_norm[j], batch=bp, seq=lp,
                                  dk=gla_dk, dv=gla_dv, mix_shape=(tp + bs, GLA_HEADS * gla_dv + DN_HEADS * DN_HEAD))
            mix, gla_s = _gla_sample(proj, small, cols, wg_pad, gla_b_gate[j], gla_norm[j], state_gla, j, gla_s, mix,
                                     row0=tp, dk=gla_dk, dv=gla_dv)
            mix, d1, c1 = _dn_prompt(proj, small, cols, dn_conv_w, j, alog_pad, dtb_pad, dn_norm[j], mix,
                                     batch=bp, seq=lp)
            mix, dn_s = _dn_sample(proj, small, cols, dn_conv_w, j, alog_pad, dtb_pad, dn_norm[j],
                                   state_delta, state_conv, dn_s, mix, row0=tp)
            c2 = jnp.concatenate([state_conv[j, :, 1:], proj[tp:, None, cols["dn_q"]:cols["dn_q"] + dn_ch]], axis=1)
            x = _matmul(mix, [ab_w_out], tn=512, out_dtype=F32, resid=x, layer=j, name="ab_out_proj")
            gla_p.append(g1)
            dn_p.append(d1)
            cv_p.append(c1); cv_s.append(c2)
        else:
            x, latent, kr2 = _mla_mixer(x, cs, norm_mix[layer], mla_w_a[j], mla_q_norm[j], mla_kv_norm[j],
                                        mla_w_qb[j], mla_w_kvb, mla_w_o, cache_latent, cache_k_rope_t,
                                        page_table, j, bp=bp, lp=lp, bs=bs)
            lat_p.append(latent[:tp].reshape(bp, lp, KV_LORA)); lat_s.append(latent[tp:].reshape(bs, 1, KV_LORA))
            kr_p.append(kr2[:tp, :QK_ROPE].reshape(bp, lp, QK_ROPE)); kr_s.append(kr2[tp:, :QK_ROPE].reshape(bs, 1, QK_ROPE))
        hid = _matmul(x, [ffn_w_gate, ffn_w_up], tn=256, out_dtype=BF16, gain=norm_ffn[layer], layer=layer,
                      name="ffn_gate_up")
        x = _matmul(hid, [ffn_w_down], tn=256, out_dtype=F32, resid=x, layer=layer, name="ffn_down")
    y_prompt = _rmsnorm_rows(x, norm_final, row0=0, n_rows=tp, rows=256).reshape(bp, lp, d_model)
    y_sample = _rmsnorm_rows(x, norm_final, row0=tp, n_rows=bs, rows=bs).reshape(bs, 1, d_model)
    return (y_prompt, y_sample, jnp.stack(gla_p), gla_s, jnp.stack(dn_p), dn_s,
            jnp.stack(cv_p), jnp.stack(cv_s), jnp.stack(lat_p), jnp.stack(lat_s), jnp.stack(kr_p), jnp.stack(kr_s))
```
